```python
import jax, jax.numpy as jnp
from jax import lax
import numpy as np

D_MODEL = 1024
BATCH = 4
SEQ = 4096
DEPTH = 2

CHUNK = 64
LEFT_CHUNKS = 8
BAND = (LEFT_CHUNKS + 1) * CHUNK
ATT_HEADS = 8
ATT_HEAD_DIM = 64
ATT_WIDTH = ATT_HEADS * ATT_HEAD_DIM
MAX_REL_DIST = 128
SGU_CHUNK = 128
SGU_GROUPS = 4
SGU_WIDTH = 512
SGU_GROUP_DIM = SGU_WIDTH // SGU_GROUPS
N_BRANCHES = 2
IN_WIDTH = 3 * ATT_WIDTH + 2 * SGU_WIDTH + N_BRANCHES * D_MODEL
D_FF = -(-(8 * D_MODEL) // (3 * 256)) * 256
EPS = 1e-6

kernel_name = "hybrid_chunk_attn_sgu_block"


def rmsnorm(x, g):
    xf = x.astype(jnp.float32)
    xf = xf * lax.rsqrt(jnp.mean(xf * xf, axis=-1, keepdims=True) + EPS)
    return (xf * g.astype(jnp.float32)).astype(x.dtype)


def layernorm(x, g, b):
    xf = x.astype(jnp.float32)
    mu = jnp.mean(xf, axis=-1, keepdims=True)
    var = jnp.mean(jnp.square(xf - mu), axis=-1, keepdims=True)
    y = (xf - mu) * lax.rsqrt(var + EPS) * g.astype(jnp.float32) + b.astype(jnp.float32)
    return y.astype(x.dtype)


def chunked_rel_attention(q, k, v, rel_bias):
    B, S, H, Dh = q.shape
    nc = S // CHUNK
    pad = LEFT_CHUNKS * CHUNK
    qc = q.reshape(B, nc, CHUNK, H, Dh)
    kp = jnp.pad(k, ((0, 0), (pad, 0), (0, 0), (0, 0)))
    vp = jnp.pad(v, ((0, 0), (pad, 0), (0, 0), (0, 0)))
    band_idx = jnp.arange(nc)[:, None] * CHUNK + jnp.arange(BAND)[None, :]
    kb = kp[:, band_idx]
    vb = vp[:, band_idx]
    scores = jnp.einsum('bcqhd,bckhd->bhcqk', qc, kb).astype(jnp.float32) * (Dh ** -0.5)
    q_pos = pad + jnp.arange(CHUNK)
    k_pos = jnp.arange(BAND)
    rel = jnp.clip(q_pos[:, None] - k_pos[None, :], -MAX_REL_DIST, MAX_REL_DIST) + MAX_REL_DIST
    bias = rel_bias.astype(jnp.float32)[:, rel]
    scores = scores + bias[None, :, None, :, :]
    key_chunk = jnp.arange(nc)[:, None] - LEFT_CHUNKS + (k_pos // CHUNK)[None, :]
    valid = key_chunk >= 0
    scores = jnp.where(valid[None, None, :, None, :], scores, jnp.float32(-1e30))
    probs = jax.nn.softmax(scores, axis=-1).astype(v.dtype)
    out = jnp.einsum('bhcqk,bckhd->bcqhd', probs, vb)
    return out.reshape(B, S, H * Dh)


def spatial_gating(u, v, ln_g, ln_b, w_s, b_s):
    B, S, _ = v.shape
    ng = S // SGU_CHUNK
    v = layernorm(v, ln_g, ln_b)
    vg = v.reshape(B, ng, SGU_CHUNK, SGU_GROUPS, SGU_GROUP_DIM)
    causal = jnp.tril(jnp.ones((SGU_CHUNK, SGU_CHUNK), dtype=bool))
    w = jnp.where(causal[None], w_s, jnp.zeros_like(w_s))
    mixed = jnp.einsum('gts,bnsgd->bntgd', w, vg) + b_s.T[None, None, :, :, None]
    return u * mixed.reshape(B, S, SGU_WIDTH)


def setup_inputs(seed: int = 0) -> dict:
    key = jax.random.key(seed)
    ks = jax.random.split(key, 16)
    f32 = jnp.float32

    def nrm(k, shape, scale):
        return jax.random.normal(k, shape, f32) * scale

    x = nrm(ks[0], (BATCH, SEQ, D_MODEL), 1.0)
    norm_mix = 1.0 + nrm(ks[1], (DEPTH, D_MODEL), 0.02)
    w_in = nrm(ks[2], (DEPTH, D_MODEL, IN_WIDTH), D_MODEL ** -0.5)
    att_rel_bias = nrm(ks[3], (DEPTH, ATT_HEADS, 2 * MAX_REL_DIST + 1), 0.1)
    sgu_norm_gain = 1.0 + nrm(ks[4], (DEPTH, SGU_WIDTH), 0.02)
    sgu_norm_bias = nrm(ks[5], (DEPTH, SGU_WIDTH), 0.02)
    sgu_w = nrm(ks[6], (DEPTH, SGU_GROUPS, SGU_CHUNK, SGU_CHUNK), 0.5 * SGU_CHUNK ** -0.5)
    sgu_b = 1.0 + nrm(ks[7], (DEPTH, SGU_GROUPS, SGU_CHUNK), 0.02)
    w_br_att = nrm(ks[8], (DEPTH, ATT_WIDTH, D_MODEL), ATT_WIDTH ** -0.5)
    w_br_sgu = nrm(ks[9], (DEPTH, SGU_WIDTH, D_MODEL), SGU_WIDTH ** -0.5)
    b_gate = nrm(ks[10], (DEPTH, N_BRANCHES, D_MODEL), 0.02)
    w_out = nrm(ks[11], (DEPTH, D_MODEL, D_MODEL), D_MODEL ** -0.5)
    norm_ffn = 1.0 + nrm(ks[12], (DEPTH, D_MODEL), 0.02)
    w_ffn_in = nrm(ks[13], (DEPTH, D_MODEL, 2 * D_FF), D_MODEL ** -0.5)
    w_ffn_out = nrm(ks[14], (DEPTH, D_FF, D_MODEL), D_FF ** -0.5)
    norm_final = 1.0 + nrm(ks[15], (D_MODEL,), 0.02)
    return {"x": x, "norm_mix": norm_mix, "w_in": w_in, "att_rel_bias": att_rel_bias,
            "sgu_norm_gain": sgu_norm_gain, "sgu_norm_bias": sgu_norm_bias,
            "sgu_w": sgu_w, "sgu_b": sgu_b, "w_br_att": w_br_att, "w_br_sgu": w_br_sgu,
            "b_gate": b_gate, "w_out": w_out, "norm_ffn": norm_ffn,
            "w_ffn_in": w_ffn_in, "w_ffn_out": w_ffn_out, "norm_final": norm_final}


def reference(x, norm_mix, w_in, att_rel_bias, sgu_norm_gain, sgu_norm_bias, sgu_w, sgu_b,
              w_br_att, w_br_sgu, b_gate, w_out, norm_ffn, w_ffn_in, w_ffn_out, norm_final):
    B, S, D = x.shape
    h = x
    for l in range(DEPTH):
        xn = rmsnorm(h, norm_mix[l])
        proj = xn @ w_in[l]
        q, k, v, u, vs, gate_logits = jnp.split(
            proj, np.cumsum([ATT_WIDTH, ATT_WIDTH, ATT_WIDTH, SGU_WIDTH, SGU_WIDTH]).tolist(), axis=-1)
        q = q.reshape(B, S, ATT_HEADS, ATT_HEAD_DIM)
        k = k.reshape(B, S, ATT_HEADS, ATT_HEAD_DIM)
        v = v.reshape(B, S, ATT_HEADS, ATT_HEAD_DIM)
        att = chunked_rel_attention(q, k, v, att_rel_bias[l])
        sgu = spatial_gating(jax.nn.gelu(u), jax.nn.gelu(vs), sgu_norm_gain[l], sgu_norm_bias[l],
                             sgu_w[l], sgu_b[l])
        br_att = att @ w_br_att[l]
        br_sgu = sgu @ w_br_sgu[l]
        gates = jax.nn.sigmoid(gate_logits.reshape(B, S, N_BRANCHES, D) + b_gate[l])
        merged = gates[:, :, 0] * br_att + gates[:, :, 1] * br_sgu
        h = h + merged @ w_out[l]
        hn = rmsnorm(h, norm_ffn[l])
        g, up = jnp.split(hn @ w_ffn_in[l], 2, axis=-1)
        h = h + (jax.nn.silu(g) * up) @ w_ffn_out[l]
    return rmsnorm(h, norm_final)
```

```python
import functools
import math

import jax
import jax.numpy as jnp
from jax import lax
from jax.experimental import pallas as pl
from jax.experimental.pallas import tpu as pltpu

D_MODEL = 1024
CHUNK = 64
LEFT_CHUNKS = 8
ATT_HEADS = 8
ATT_HEAD_DIM = 64
ATT_WIDTH = ATT_HEADS * ATT_HEAD_DIM
MAX_REL_DIST = 128
SGU_CHUNK = 128
SGU_GROUPS = 4
SGU_WIDTH = 512
SGU_GROUP_DIM = SGU_WIDTH // SGU_GROUPS
N_BRANCHES = 2
GATE_WIDTH = N_BRANCHES * D_MODEL
QKV_WIDTH = 3 * ATT_WIDTH
UV_WIDTH = 2 * SGU_WIDTH
IN_WIDTH = QKV_WIDTH + UV_WIDTH + GATE_WIDTH
D_FF = 2816
EPS = 1e-6
MASK_VALUE = -1e30

V7X_VMEM_BYTES = 64 * 1024 * 1024
VMEM_LIMIT_BYTES = V7X_VMEM_BYTES - 8 * 1024 * 1024
LANES = 128

ROW_BLOCK = 512
ATT_QBLOCK = 2 * CHUNK
ATT_WINDOW = (LEFT_CHUNKS + 2) * CHUNK
PAST = LEFT_CHUNKS * CHUNK


def _params(n_axes):
    return pltpu.CompilerParams(
        dimension_semantics=("arbitrary",) * n_axes,
        vmem_limit_bytes=VMEM_LIMIT_BYTES)


def _rmsnorm_f32(x, g):
    return x * lax.rsqrt(jnp.mean(x * x, axis=-1, keepdims=True) + EPS) * g


def _gelu_tanh(x):
    c = math.sqrt(2.0 / math.pi)
    return 0.5 * x * (1.0 + jnp.tanh(c * (x + 0.044715 * (x * x * x))))


def _resident(shape):
    return pl.BlockSpec(shape, lambda *_: (0,) * len(shape), pipeline_mode=pl.Buffered(1))


def _proj_kernel(x_ref, g_ref, w_ref, qkv_ref, uv_ref, gate_ref):
    xn = _rmsnorm_f32(x_ref[...], g_ref[...]).astype(jnp.bfloat16)
    for out_ref, start in ((qkv_ref, 0), (uv_ref, QKV_WIDTH), (gate_ref, QKV_WIDTH + UV_WIDTH)):
        width = out_ref.shape[1]
        for c in range(0, width, 512):
            out_ref[:, c:c + 512] = jnp.dot(
                xn, w_ref[:, start + c:start + c + 512],
                preferred_element_type=jnp.float32).astype(out_ref.dtype)


def _proj(h, g, w):
    t = h.shape[0]
    row = lambda i: (i, 0)
    return pl.pallas_call(
        _proj_kernel,
        grid=(t // ROW_BLOCK,),
        in_specs=[pl.BlockSpec((ROW_BLOCK, D_MODEL), row),
                  _resident((1, D_MODEL)),
                  _resident((D_MODEL, IN_WIDTH))],
        out_specs=[pl.BlockSpec((ROW_BLOCK, QKV_WIDTH), row),
                   pl.BlockSpec((ROW_BLOCK, UV_WIDTH), row),
                   pl.BlockSpec((ROW_BLOCK, GATE_WIDTH), row)],
        out_shape=[jax.ShapeDtypeStruct((t, QKV_WIDTH), jnp.bfloat16),
                   jax.ShapeDtypeStruct((t, UV_WIDTH), jnp.bfloat16),
                   jax.ShapeDtypeStruct((t, GATE_WIDTH), jnp.bfloat16)],
        compiler_params=_params(1),
        name="proj",
    )(h, g, w)


def _attn_kernel(q_ref, kp_ref, kc_ref, vp_ref, vc_ref, bias_ref, o_ref, kwin, vwin):
    kwin[0:ROW_BLOCK, :] = kp_ref[...]
    kwin[ROW_BLOCK:2 * ROW_BLOCK, :] = kc_ref[...]
    vwin[0:ROW_BLOCK, :] = vp_ref[...]
    vwin[ROW_BLOCK:2 * ROW_BLOCK, :] = vc_ref[...]
    first_block = pl.program_id(1) == 0
    lane = lax.broadcasted_iota(jnp.int32, (ATT_QBLOCK, LANES), 1)
    low_half = lane < ATT_HEAD_DIM
    col = lax.broadcasted_iota(jnp.int32, (1, ATT_WINDOW), 1)
    scale = jnp.bfloat16(ATT_HEAD_DIM ** -0.5)

    def sub_block(j, carry):
        q0 = pl.multiple_of(j * ATT_QBLOCK, ATT_QBLOCK)
        pen = jnp.where(first_block & (col < PAST - j * ATT_QBLOCK), MASK_VALUE, 0.0)
        for hp in range(ATT_HEADS // 2):
            cols = slice(hp * LANES, (hp + 1) * LANES)
            qp = q_ref[pl.ds(q0, ATT_QBLOCK), cols] * scale
            kw = kwin[pl.ds(q0, ATT_WINDOW), cols]
            vw = vwin[pl.ds(q0, ATT_WINDOW), cols]
            outs = []
            for half in range(2):
                keep = low_half if half == 0 else jnp.logical_not(low_half)
                qh = jnp.where(keep, qp, jnp.zeros_like(qp))
                s = lax.dot_general(qh, kw, (((1,), (1,)), ((), ())),
                                    preferred_element_type=jnp.float32)
                s = s + bias_ref[2 * hp + half] + pen
                m = jnp.max(s, axis=-1, keepdims=True)
                e = jnp.exp(s - m)
                l = jnp.sum(e, axis=-1, keepdims=True)
                o = jnp.dot(e.astype(jnp.bfloat16), vw, preferred_element_type=jnp.float32)
                outs.append(o * (1.0 / l))
            o_ref[pl.ds(q0, ATT_QBLOCK), cols] = jnp.where(low_half, outs[0], outs[1]).astype(o_ref.dtype)
        return carry

    lax.fori_loop(0, ROW_BLOCK // ATT_QBLOCK, sub_block, 0)


def _attention(qkv, bias, batch, seq):
    t = qkv.shape[0]
    nb = seq // ROW_BLOCK
    cur = lambda col: (lambda b, i: (b * nb + i, col))
    prev = lambda col: (lambda b, i: (b * nb + jnp.maximum(i - 1, 0), col))
    blk = (ROW_BLOCK, ATT_WIDTH)
    return pl.pallas_call(
        _attn_kernel,
        grid=(batch, nb),
        in_specs=[pl.BlockSpec(blk, cur(0)),
                  pl.BlockSpec(blk, prev(1)), pl.BlockSpec(blk, cur(1)),
                  pl.BlockSpec(blk, prev(2)), pl.BlockSpec(blk, cur(2)),
                  _resident((ATT_HEADS, ATT_QBLOCK, ATT_WINDOW))],
        out_specs=pl.BlockSpec(blk, cur(0)),
        out_shape=jax.ShapeDtypeStruct((t, ATT_WIDTH), jnp.bfloat16),
        scratch_shapes=[pltpu.VMEM((2 * ROW_BLOCK, ATT_WIDTH), jnp.bfloat16),
                        pltpu.VMEM((2 * ROW_BLOCK, ATT_WIDTH), jnp.bfloat16)],
        compiler_params=_params(2),
        name="attention",
    )(qkv, qkv, qkv, qkv, qkv, bias)


def _attention_bias(rel_bias):
    r = jnp.arange(ATT_QBLOCK)[:, None]
    c = jnp.arange(ATT_WINDOW)[None, :]
    rel = jnp.clip(PAST + r - c, -MAX_REL_DIST, MAX_REL_DIST) + MAX_REL_DIST
    qc, kc = r // CHUNK, c // CHUNK
    visible = (kc >= qc) & (kc <= qc + LEFT_CHUNKS)
    return jnp.where(visible[None], rel_bias.astype(jnp.float32)[:, rel], MASK_VALUE)


def _sgu_kernel(u_ref, v_ref, lng_ref, lnb_ref, w_ref, b_ref, o_ref):
    t_to = lax.broadcasted_iota(jnp.int32, (SGU_CHUNK, SGU_CHUNK), 0)
    t_from = lax.broadcasted_iota(jnp.int32, (SGU_CHUNK, SGU_CHUNK), 1)
    causal = t_to >= t_from
    ws = [jnp.where(causal, w_ref[g], 0.0).astype(jnp.bfloat16) for g in range(SGU_GROUPS)]
    for c in range(ROW_BLOCK // SGU_CHUNK):
        rows = slice(c * SGU_CHUNK, (c + 1) * SGU_CHUNK)
        gu = _gelu_tanh(u_ref[rows, :].astype(jnp.float32))
        gv = _gelu_tanh(v_ref[rows, :].astype(jnp.float32))
        mu = jnp.mean(gv, axis=-1, keepdims=True)
        d = gv - mu
        var = jnp.mean(d * d, axis=-1, keepdims=True)
        vn = (d * lax.rsqrt(var + EPS) * lng_ref[...] + lnb_ref[...]).astype(jnp.bfloat16)
        for g in range(SGU_GROUPS):
            cols = slice(g * SGU_GROUP_DIM, (g + 1) * SGU_GROUP_DIM)
            mixed = jnp.dot(ws[g], vn[:, cols], preferred_element_type=jnp.float32) + b_ref[:, cols]
            o_ref[rows, cols] = (gu[:, cols] * mixed).astype(o_ref.dtype)


def _sgu(uv, ln_g, ln_b, w, b_full):
    t = uv.shape[0]
    blk = (ROW_BLOCK, SGU_WIDTH)
    return pl.pallas_call(
        _sgu_kernel,
        grid=(t // ROW_BLOCK,),
        in_specs=[pl.BlockSpec(blk, lambda i: (i, 0)),
                  pl.BlockSpec(blk, lambda i: (i, 1)),
                  _resident((1, SGU_WIDTH)), _resident((1, SGU_WIDTH)),
                  _resident((SGU_GROUPS, SGU_CHUNK, SGU_CHUNK)),
                  _resident((SGU_CHUNK, SGU_WIDTH))],
        out_specs=pl.BlockSpec(blk, lambda i: (i, 0)),
        out_shape=jax.ShapeDtypeStruct((t, SGU_WIDTH), jnp.bfloat16),
        compiler_params=_params(1),
        name="sgu",
    )(uv, uv, ln_g, ln_b, w, b_full)


def _merge_kernel(att_ref, sgu_ref, gate_ref, h_ref, wa_ref, ws_ref, bg_ref, wo_ref, o_ref):
    br_att = jnp.dot(att_ref[...], wa_ref[...], preferred_element_type=jnp.float32)
    br_sgu = jnp.dot(sgu_ref[...], ws_ref[...], preferred_element_type=jnp.float32)
    g_att = jax.nn.sigmoid(gate_ref[:, 0:D_MODEL].astype(jnp.float32) + bg_ref[0:1, :])
    g_sgu = jax.nn.sigmoid(gate_ref[:, D_MODEL:GATE_WIDTH].astype(jnp.float32) + bg_ref[1:2, :])
    merged = (g_att * br_att + g_sgu * br_sgu).astype(jnp.bfloat16)
    o_ref[...] = h_ref[...] + jnp.dot(merged, wo_ref[...], preferred_element_type=jnp.float32)


def _merge(att, sgu, gate, h, wa, ws, bg, wo):
    t = h.shape[0]
    row = lambda i: (i, 0)
    return pl.pallas_call(
        _merge_kernel,
        grid=(t // ROW_BLOCK,),
        in_specs=[pl.BlockSpec((ROW_BLOCK, ATT_WIDTH), row),
                  pl.BlockSpec((ROW_BLOCK, SGU_WIDTH), row),
                  pl.BlockSpec((ROW_BLOCK, GATE_WIDTH), row),
                  pl.BlockSpec((ROW_BLOCK, D_MODEL), row),
                  _resident((ATT_WIDTH, D_MODEL)), _resident((SGU_WIDTH, D_MODEL)),
                  _resident((N_BRANCHES, D_MODEL)), _resident((D_MODEL, D_MODEL))],
        out_specs=pl.BlockSpec((ROW_BLOCK, D_MODEL), row),
        out_shape=jax.ShapeDtypeStruct((t, D_MODEL), jnp.float32),
        compiler_params=_params(1),
        name="merge",
    )(att, sgu, gate, h, wa, ws, bg, wo)


def _ffn_kernel(h_ref, g_ref, wi_ref, wo_ref, gf_ref, o_ref, *, final_norm):
    h = h_ref[...]
    hn = _rmsnorm_f32(h, g_ref[...]).astype(jnp.bfloat16)
    gate = jnp.dot(hn, wi_ref[:, 0:D_FF], preferred_element_type=jnp.float32)
    up = jnp.dot(hn, wi_ref[:, D_FF:2 * D_FF], preferred_element_type=jnp.float32)
    act = (gate * jax.nn.sigmoid(gate) * up).astype(jnp.bfloat16)
    out = h + jnp.dot(act, wo_ref[...], preferred_element_type=jnp.float32)
    if final_norm:
        out = _rmsnorm_f32(out, gf_ref[...])
    o_ref[...] = out


def _ffn(h, g, wi, wo, g_final, final_norm):
    t = h.shape[0]
    row = lambda i: (i, 0)
    return pl.pallas_call(
        functools.partial(_ffn_kernel, final_norm=final_norm),
        grid=(t // ROW_BLOCK,),
        in_specs=[pl.BlockSpec((ROW_BLOCK, D_MODEL), row),
                  _resident((1, D_MODEL)),
                  _resident((D_MODEL, 2 * D_FF)), _resident((D_FF, D_MODEL)),
                  _resident((1, D_MODEL))],
        out_specs=pl.BlockSpec((ROW_BLOCK, D_MODEL), row),
        out_shape=jax.ShapeDtypeStruct((t, D_MODEL), jnp.float32),
        compiler_params=_params(1),
        name="ffn_final" if final_norm else "ffn",
    )(h, g, wi, wo, g_final)


def kernel(x, norm_mix, w_in, att_rel_bias, sgu_norm_gain, sgu_norm_bias, sgu_w, sgu_b, w_br_att, w_br_sgu, b_gate, w_out, norm_ffn, w_ffn_in, w_ffn_out, norm_final):
    batch, seq, d = x.shape
    depth = w_in.shape[0]
    assert d == D_MODEL and seq % ROW_BLOCK == 0
    bf16 = jnp.bfloat16
    h = x.reshape(batch * seq, d)
    for l in range(depth):
        qkv, uv, gate = _proj(h, norm_mix[l][None], w_in[l].astype(bf16))
        att = _attention(qkv, _attention_bias(att_rel_bias[l]), batch, seq)
        b_full = jnp.repeat(sgu_b[l].T, SGU_GROUP_DIM, axis=1)
        sgu = _sgu(uv, sgu_norm_gain[l][None], sgu_norm_bias[l][None], sgu_w[l], b_full)
        h = _merge(att, sgu, gate, h, w_br_att[l].astype(bf16), w_br_sgu[l].astype(bf16),
                   b_gate[l], w_out[l].astype(bf16))
        h = _ffn(h, norm_ffn[l][None], w_ffn_in[l].astype(bf16), w_ffn_out[l].astype(bf16),
                 norm_final[None], final_norm=(l == depth - 1))
    return h.reshape(batch, seq, d)
```

```python
import functools
import math

import jax
import jax.numpy as jnp
from jax import lax
from jax.experimental import pallas as pl
from jax.experimental.pallas import tpu as pltpu

D_MODEL = 1024
CHUNK = 64
LEFT_CHUNKS = 8
ATT_HEADS = 8
ATT_HEAD_DIM = 64
ATT_WIDTH = ATT_HEADS * ATT_HEAD_DIM
MAX_REL_DIST = 128
SGU_CHUNK = 128
SGU_GROUPS = 4
SGU_WIDTH = 512
SGU_GROUP_DIM = SGU_WIDTH // SGU_GROUPS
N_BRANCHES = 2
GATE_WIDTH = N_BRANCHES * D_MODEL
QKV_WIDTH = 3 * ATT_WIDTH
UV_WIDTH = 2 * SGU_WIDTH
IN_WIDTH = QKV_WIDTH + UV_WIDTH + GATE_WIDTH
D_FF = 2816
EPS = 1e-6
MASK_VALUE = -1e30

V7X_VMEM_BYTES = 64 * 1024 * 1024
VMEM_LIMIT_BYTES = V7X_VMEM_BYTES - 8 * 1024 * 1024
LANES = 128

ROW_BLOCK = 512
ATT_QBLOCK = 2 * CHUNK
ATT_WINDOW = (LEFT_CHUNKS + 2) * CHUNK
PAST = LEFT_CHUNKS * CHUNK
EXT_WIDTH = ATT_WINDOW + ATT_QBLOCK


def _params(n_axes):
    return pltpu.CompilerParams(
        dimension_semantics=("arbitrary",) * n_axes,
        vmem_limit_bytes=VMEM_LIMIT_BYTES)


def _rmsnorm_f32(x, g):
    return x * lax.rsqrt(jnp.mean(x * x, axis=-1, keepdims=True) + EPS) * g


def _gelu_tanh(x):
    c = math.sqrt(2.0 / math.pi)
    return 0.5 * x * (1.0 + jnp.tanh(c * (x + 0.044715 * (x * x * x))))


def _resident(shape):
    return pl.BlockSpec(shape, lambda *_: (0,) * len(shape), pipeline_mode=pl.Buffered(1))


def _proj_kernel(x_ref, g_ref, w_ref, qkv_ref, uv_ref, gate_ref):
    xn = _rmsnorm_f32(x_ref[...], g_ref[...]).astype(jnp.bfloat16)
    for out_ref, start in ((qkv_ref, 0), (uv_ref, QKV_WIDTH), (gate_ref, QKV_WIDTH + UV_WIDTH)):
        width = out_ref.shape[1]
        for c in range(0, width, 512):
            out_ref[:, c:c + 512] = jnp.dot(
                xn, w_ref[:, start + c:start + c + 512],
                preferred_element_type=jnp.float32).astype(out_ref.dtype)


def _proj(h, g, w):
    t = h.shape[0]
    row = lambda i: (i, 0)
    return pl.pallas_call(
        _proj_kernel,
        grid=(t // ROW_BLOCK,),
        in_specs=[pl.BlockSpec((ROW_BLOCK, D_MODEL), row),
                  _resident((1, D_MODEL)),
                  _resident((D_MODEL, IN_WIDTH))],
        out_specs=[pl.BlockSpec((ROW_BLOCK, QKV_WIDTH), row),
                   pl.BlockSpec((ROW_BLOCK, UV_WIDTH), row),
                   pl.BlockSpec((ROW_BLOCK, GATE_WIDTH), row)],
        out_shape=[jax.ShapeDtypeStruct((t, QKV_WIDTH), jnp.bfloat16),
                   jax.ShapeDtypeStruct((t, UV_WIDTH), jnp.bfloat16),
                   jax.ShapeDtypeStruct((t, GATE_WIDTH), jnp.bfloat16)],
        compiler_params=_params(1),
        name="proj",
    )(h, g, w)


SOFTMAX_ROWS = 16


def _attn_kernel(q_ref, kp_ref, kc_ref, vp_ref, vc_ref, ext_ref, o_ref,
                 kwin, vwin, bias_scr, s_scr, e_scr, l_scr):
    @pl.when((pl.program_id(0) == 0) & (pl.program_id(1) == 0))
    def _expand_bias():
        r = lax.broadcasted_iota(jnp.int32, (ATT_QBLOCK, ATT_WINDOW), 0)
        c = lax.broadcasted_iota(jnp.int32, (ATT_QBLOCK, ATT_WINDOW), 1)
        qc, kc = r // CHUNK, c // CHUNK
        visible = (kc >= qc) & (kc <= qc + LEFT_CHUNKS)
        for h in range(ATT_HEADS):
            rows = jnp.broadcast_to(ext_ref[h], (ATT_QBLOCK, EXT_WIDTH))
            toeplitz = pltpu.roll(rows, EXT_WIDTH - (ATT_QBLOCK - 1), 1, stride=1, stride_axis=0)
            bias_scr[h] = jnp.where(visible, toeplitz[:, 0:ATT_WINDOW], MASK_VALUE)

    kwin[0:ROW_BLOCK, :] = kp_ref[...]
    kwin[ROW_BLOCK:2 * ROW_BLOCK, :] = kc_ref[...]
    vwin[0:ROW_BLOCK, :] = vp_ref[...]
    vwin[ROW_BLOCK:2 * ROW_BLOCK, :] = vc_ref[...]
    first_block = pl.program_id(1) == 0
    lane = lax.broadcasted_iota(jnp.int32, (ATT_QBLOCK, LANES), 1)
    low_half = lane < ATT_HEAD_DIM
    col = lax.broadcasted_iota(jnp.int32, (1, ATT_WINDOW), 1)
    scale = jnp.bfloat16(ATT_HEAD_DIM ** -0.5)

    def sub_block(j, carry):
        q0 = pl.multiple_of(j * ATT_QBLOCK, ATT_QBLOCK)
        for hp in range(ATT_HEADS // 2):
            cols = slice(hp * LANES, (hp + 1) * LANES)
            qp = q_ref[pl.ds(q0, ATT_QBLOCK), cols] * scale
            kw = kwin[pl.ds(q0, ATT_WINDOW), cols]
            for half in range(2):
                h = 2 * hp + half
                keep = low_half if half == 0 else jnp.logical_not(low_half)
                qh = jnp.where(keep, qp, jnp.zeros_like(qp))
                s = lax.dot_general(qh, kw, (((1,), (1,)), ((), ())),
                                    preferred_element_type=jnp.float32)
                s_scr[h] = s + bias_scr[h]

        @pl.when(first_block)
        def _mask_missing_keys():
            pen = jnp.where(col < PAST - j * ATT_QBLOCK, MASK_VALUE, 0.0)
            for h in range(ATT_HEADS):
                s_scr[h] = s_scr[h] + pen

        for h in range(ATT_HEADS):
            for g in range(ATT_QBLOCK // SOFTMAX_ROWS):
                rows = slice(g * SOFTMAX_ROWS, (g + 1) * SOFTMAX_ROWS)
                s = s_scr[h, rows, :]
                e = jnp.exp(s - jnp.max(s, axis=-1, keepdims=True))
                e_scr[h, rows, :] = e.astype(jnp.bfloat16)
                l_scr[h, rows, :] = jnp.broadcast_to(
                    jnp.sum(e, axis=-1, keepdims=True), (SOFTMAX_ROWS, LANES))

        for hp in range(ATT_HEADS // 2):
            cols = slice(hp * LANES, (hp + 1) * LANES)
            vw = vwin[pl.ds(q0, ATT_WINDOW), cols]
            o_lo = jnp.dot(e_scr[2 * hp], vw, preferred_element_type=jnp.float32)
            o_hi = jnp.dot(e_scr[2 * hp + 1], vw, preferred_element_type=jnp.float32)
            denom = jnp.where(low_half, l_scr[2 * hp], l_scr[2 * hp + 1])
            o = jnp.where(low_half, o_lo, o_hi) * (1.0 / denom)
            o_ref[pl.ds(q0, ATT_QBLOCK), cols] = o.astype(o_ref.dtype)
        return carry

    lax.fori_loop(0, ROW_BLOCK // ATT_QBLOCK, sub_block, 0)


def _attention(qkv, ext, batch, seq):
    t = qkv.shape[0]
    nb = seq // ROW_BLOCK
    cur = lambda col: (lambda b, i: (b * nb + i, col))
    prev = lambda col: (lambda b, i: (b * nb + jnp.maximum(i - 1, 0), col))
    blk = (ROW_BLOCK, ATT_WIDTH)
    return pl.pallas_call(
        _attn_kernel,
        grid=(batch, nb),
        in_specs=[pl.BlockSpec(blk, cur(0)),
                  pl.BlockSpec(blk, prev(1)), pl.BlockSpec(blk, cur(1)),
                  pl.BlockSpec(blk, prev(2)), pl.BlockSpec(blk, cur(2)),
                  _resident((ATT_HEADS, 1, EXT_WIDTH))],
        out_specs=pl.BlockSpec(blk, cur(0)),
        out_shape=jax.ShapeDtypeStruct((t, ATT_WIDTH), jnp.bfloat16),
        scratch_shapes=[pltpu.VMEM((2 * ROW_BLOCK, ATT_WIDTH), jnp.bfloat16),
                        pltpu.VMEM((2 * ROW_BLOCK, ATT_WIDTH), jnp.bfloat16),
                        pltpu.VMEM((ATT_HEADS, ATT_QBLOCK, ATT_WINDOW), jnp.float32),
                        pltpu.VMEM((ATT_HEADS, ATT_QBLOCK, ATT_WINDOW), jnp.float32),
                        pltpu.VMEM((ATT_HEADS, ATT_QBLOCK, ATT_WINDOW), jnp.bfloat16),
                        pltpu.VMEM((ATT_HEADS, ATT_QBLOCK, LANES), jnp.float32)],
        compiler_params=_params(2),
        name="attention",
    )(qkv, qkv, qkv, qkv, qkv, ext)


def _extended_rel_bias(rel_bias):
    d_first = PAST + ATT_QBLOCK - 1
    d_last = d_first - (EXT_WIDTH - 1)
    assert d_first >= MAX_REL_DIST and d_last == -MAX_REL_DIST
    n_clipped = d_first - MAX_REL_DIST
    table = rel_bias.astype(jnp.float32)
    top = jnp.broadcast_to(table[:, 2 * MAX_REL_DIST:], (ATT_HEADS, n_clipped))
    return jnp.concatenate([top, table[:, ::-1]], axis=1)[:, None, :]


def _sgu_kernel(u_ref, v_ref, lng_ref, lnb_ref, w_ref, b_ref, o_ref):
    t_to = lax.broadcasted_iota(jnp.int32, (SGU_CHUNK, SGU_CHUNK), 0)
    t_from = lax.broadcasted_iota(jnp.int32, (SGU_CHUNK, SGU_CHUNK), 1)
    causal = t_to >= t_from
    ws = [jnp.where(causal, w_ref[g], 0.0).astype(jnp.bfloat16) for g in range(SGU_GROUPS)]
    for c in range(ROW_BLOCK // SGU_CHUNK):
        rows = slice(c * SGU_CHUNK, (c + 1) * SGU_CHUNK)
        gu = _gelu_tanh(u_ref[rows, :].astype(jnp.float32))
        gv = _gelu_tanh(v_ref[rows, :].astype(jnp.float32))
        mu = jnp.mean(gv, axis=-1, keepdims=True)
        d = gv - mu
        var = jnp.mean(d * d, axis=-1, keepdims=True)
        vn = (d * lax.rsqrt(var + EPS) * lng_ref[...] + lnb_ref[...]).astype(jnp.bfloat16)
        for g in range(SGU_GROUPS):
            cols = slice(g * SGU_GROUP_DIM, (g + 1) * SGU_GROUP_DIM)
            mixed = jnp.dot(ws[g], vn[:, cols], preferred_element_type=jnp.float32) + b_ref[:, cols]
            o_ref[rows, cols] = (gu[:, cols] * mixed).astype(o_ref.dtype)


def _sgu(uv, ln_g, ln_b, w, b_full):
    t = uv.shape[0]
    blk = (ROW_BLOCK, SGU_WIDTH)
    return pl.pallas_call(
        _sgu_kernel,
        grid=(t // ROW_BLOCK,),
        in_specs=[pl.BlockSpec(blk, lambda i: (i, 0)),
                  pl.BlockSpec(blk, lambda i: (i, 1)),
                  _resident((1, SGU_WIDTH)), _resident((1, SGU_WIDTH)),
                  _resident((SGU_GROUPS, SGU_CHUNK, SGU_CHUNK)),
                  _resident((SGU_CHUNK, SGU_WIDTH))],
        out_specs=pl.BlockSpec(blk, lambda i: (i, 0)),
        out_shape=jax.ShapeDtypeStruct((t, SGU_WIDTH), jnp.bfloat16),
        compiler_params=_params(1),
        name="sgu",
    )(uv, uv, ln_g, ln_b, w, b_full)


def _merge_kernel(att_ref, sgu_ref, gate_ref, h_ref, wa_ref, ws_ref, bg_ref, wo_ref, o_ref):
    br_att = jnp.dot(att_ref[...], wa_ref[...], preferred_element_type=jnp.float32)
    br_sgu = jnp.dot(sgu_ref[...], ws_ref[...], preferred_element_type=jnp.float32)
    g_att = jax.nn.sigmoid(gate_ref[:, 0:D_MODEL].astype(jnp.float32) + bg_ref[0:1, :])
    g_sgu = jax.nn.sigmoid(gate_ref[:, D_MODEL:GATE_WIDTH].astype(jnp.float32) + bg_ref[1:2, :])
    merged = (g_att * br_att + g_sgu * br_sgu).astype(jnp.bfloat16)
    o_ref[...] = h_ref[...] + jnp.dot(merged, wo_ref[...], preferred_element_type=jnp.float32)


def _merge(att, sgu, gate, h, wa, ws, bg, wo):
    t = h.shape[0]
    row = lambda i: (i, 0)
    return pl.pallas_call(
        _merge_kernel,
        grid=(t // ROW_BLOCK,),
        in_specs=[pl.BlockSpec((ROW_BLOCK, ATT_WIDTH), row),
                  pl.BlockSpec((ROW_BLOCK, SGU_WIDTH), row),
                  pl.BlockSpec((ROW_BLOCK, GATE_WIDTH), row),
                  pl.BlockSpec((ROW_BLOCK, D_MODEL), row),
                  _resident((ATT_WIDTH, D_MODEL)), _resident((SGU_WIDTH, D_MODEL)),
                  _resident((N_BRANCHES, D_MODEL)), _resident((D_MODEL, D_MODEL))],
        out_specs=pl.BlockSpec((ROW_BLOCK, D_MODEL), row),
        out_shape=jax.ShapeDtypeStruct((t, D_MODEL), jnp.float32),
        compiler_params=_params(1),
        name="merge",
    )(att, sgu, gate, h, wa, ws, bg, wo)


def _ffn_kernel(h_ref, g_ref, wi_ref, wo_ref, gf_ref, o_ref, *, final_norm):
    h = h_ref[...]
    hn = _rmsnorm_f32(h, g_ref[...]).astype(jnp.bfloat16)
    gate = jnp.dot(hn, wi_ref[:, 0:D_FF], preferred_element_type=jnp.float32)
    up = jnp.dot(hn, wi_ref[:, D_FF:2 * D_FF], preferred_element_type=jnp.float32)
    act = (gate * jax.nn.sigmoid(gate) * up).astype(jnp.bfloat16)
    out = h + jnp.dot(act, wo_ref[...], preferred_element_type=jnp.float32)
    if final_norm:
        out = _rmsnorm_f32(out, gf_ref[...])
    o_ref[...] = out


def _ffn(h, g, wi, wo, g_final, final_norm):
    t = h.shape[0]
    row = lambda i: (i, 0)
    return pl.pallas_call(
        functools.partial(_ffn_kernel, final_norm=final_norm),
        grid=(t // ROW_BLOCK,),
        in_specs=[pl.BlockSpec((ROW_BLOCK, D_MODEL), row),
                  _resident((1, D_MODEL)),
                  _resident((D_MODEL, 2 * D_FF)), _resident((D_FF, D_MODEL)),
                  _resident((1, D_MODEL))],
        out_specs=pl.BlockSpec((ROW_BLOCK, D_MODEL), row),
        out_shape=jax.ShapeDtypeStruct((t, D_MODEL), jnp.float32),
        compiler_params=_params(1),
        name="ffn_final" if final_norm else "ffn",
    )(h, g, wi, wo, g_final)


def kernel(x, norm_mix, w_in, att_rel_bias, sgu_norm_gain, sgu_norm_bias, sgu_w, sgu_b, w_br_att, w_br_sgu, b_gate, w_out, norm_ffn, w_ffn_in, w_ffn_out, norm_final):
    batch, seq, d = x.shape
    depth = w_in.shape[0]
    assert d == D_MODEL and seq % ROW_BLOCK == 0
    bf16 = jnp.bfloat16
    h = x.reshape(batch * seq, d)
    for l in range(depth):
        qkv, uv, gate = _proj(h, norm_mix[l][None], w_in[l].astype(bf16))
        att = _attention(qkv, _extended_rel_bias(att_rel_bias[l]), batch, seq)
        b_full = jnp.repeat(sgu_b[l].T, SGU_GROUP_DIM, axis=1)
        sgu = _sgu(uv, sgu_norm_gain[l][None], sgu_norm_bias[l][None], sgu_w[l], b_full)
        h = _merge(att, sgu, gate, h, w_br_att[l].astype(bf16), w_br_sgu[l].astype(bf16),
                   b_gate[l], w_out[l].astype(bf16))
        h = _ffn(h, norm_ffn[l][None], w_ffn_in[l].astype(bf16), w_ffn_out[l].astype(bf16),
                 norm_final[None], final_norm=(l == depth - 1))
    return h.reshape(batch, seq, d)
```

```python
import functools
import math

import jax
import jax.numpy as jnp
from jax import lax
from jax.experimental import pallas as pl
from jax.experimental.pallas import tpu as pltpu

D_MODEL = 1024
CHUNK = 64
LEFT_CHUNKS = 8
ATT_HEADS = 8
ATT_HEAD_DIM = 64
ATT_WIDTH = ATT_HEADS * ATT_HEAD_DIM
MAX_REL_DIST = 128
SGU_CHUNK = 128
SGU_GROUPS = 4
SGU_WIDTH = 512
SGU_GROUP_DIM = SGU_WIDTH // SGU_GROUPS
N_BRANCHES = 2
GATE_WIDTH = N_BRANCHES * D_MODEL
QKV_WIDTH = 3 * ATT_WIDTH
UV_WIDTH = 2 * SGU_WIDTH
IN_WIDTH = QKV_WIDTH + UV_WIDTH + GATE_WIDTH
D_FF = 2816
EPS = 1e-6
MASK_VALUE = -1e30

V7X_VMEM_BYTES = 64 * 1024 * 1024
VMEM_LIMIT_BYTES = V7X_VMEM_BYTES - 8 * 1024 * 1024
LANES = 128

ROW_BLOCK = 512
ATT_QBLOCK = 2 * CHUNK
ATT_WINDOW = (LEFT_CHUNKS + 2) * CHUNK
PAST = LEFT_CHUNKS * CHUNK
EXT_WIDTH = ATT_WINDOW + ATT_QBLOCK


def _params(n_axes, flags=None):
    return pltpu.CompilerParams(
        dimension_semantics=("arbitrary",) * n_axes,
        vmem_limit_bytes=VMEM_LIMIT_BYTES,
        flags=flags)


def _rmsnorm_f32(x, g):
    return x * lax.rsqrt(jnp.mean(x * x, axis=-1, keepdims=True) + EPS) * g


def _gelu_tanh(x):
    c = math.sqrt(2.0 / math.pi)
    return 0.5 * x * (1.0 + jnp.tanh(c * (x + 0.044715 * (x * x * x))))


def _resident(shape):
    return pl.BlockSpec(shape, lambda *_: (0,) * len(shape), pipeline_mode=pl.Buffered(1))


def _proj_kernel(x_ref, g_ref, w_ref, qkv_ref, uv_ref, gate_ref):
    xn = _rmsnorm_f32(x_ref[...], g_ref[...]).astype(jnp.bfloat16)
    for out_ref, start in ((qkv_ref, 0), (uv_ref, QKV_WIDTH), (gate_ref, QKV_WIDTH + UV_WIDTH)):
        width = out_ref.shape[1]
        for c in range(0, width, 512):
            out_ref[:, c:c + 512] = jnp.dot(
                xn, w_ref[:, start + c:start + c + 512],
                preferred_element_type=jnp.float32).astype(out_ref.dtype)


def _proj(h, g, w):
    t = h.shape[0]
    row = lambda i: (i, 0)
    return pl.pallas_call(
        _proj_kernel,
        grid=(t // ROW_BLOCK,),
        in_specs=[pl.BlockSpec((ROW_BLOCK, D_MODEL), row),
                  _resident((1, D_MODEL)),
                  _resident((D_MODEL, IN_WIDTH))],
        out_specs=[pl.BlockSpec((ROW_BLOCK, QKV_WIDTH), row),
                   pl.BlockSpec((ROW_BLOCK, UV_WIDTH), row),
                   pl.BlockSpec((ROW_BLOCK, GATE_WIDTH), row)],
        out_shape=[jax.ShapeDtypeStruct((t, QKV_WIDTH), jnp.bfloat16),
                   jax.ShapeDtypeStruct((t, UV_WIDTH), jnp.bfloat16),
                   jax.ShapeDtypeStruct((t, GATE_WIDTH), jnp.bfloat16)],
        compiler_params=_params(1),
        name="proj",
    )(h, g, w)


SOFTMAX_ROWS = 16
N_SUB = ROW_BLOCK // ATT_QBLOCK


def _attn_kernel(q_ref, kp_ref, kc_ref, vp_ref, vc_ref, ext_ref, o_ref,
                 kwin, vwin, bias_scr, s_scr, e_scr, l_scr):
    @pl.when((pl.program_id(0) == 0) & (pl.program_id(1) == 0))
    def _expand_bias():
        r = lax.broadcasted_iota(jnp.int32, (ATT_QBLOCK, ATT_WINDOW), 0)
        c = lax.broadcasted_iota(jnp.int32, (ATT_QBLOCK, ATT_WINDOW), 1)
        qc, kc = r // CHUNK, c // CHUNK
        visible = (kc >= qc) & (kc <= qc + LEFT_CHUNKS)
        for h in range(ATT_HEADS):
            rows = jnp.broadcast_to(ext_ref[h], (ATT_QBLOCK, EXT_WIDTH))
            toeplitz = pltpu.roll(rows, EXT_WIDTH - (ATT_QBLOCK - 1), 1, stride=1, stride_axis=0)
            base = jnp.where(visible, toeplitz[:, 0:ATT_WINDOW], MASK_VALUE)
            bias_scr[N_SUB, h] = base
            for j in range(N_SUB):
                bias_scr[j, h] = jnp.where(c < PAST - j * ATT_QBLOCK, MASK_VALUE, base)

    kwin[0:ROW_BLOCK, :] = kp_ref[...]
    kwin[ROW_BLOCK:2 * ROW_BLOCK, :] = kc_ref[...]
    vwin[0:ROW_BLOCK, :] = vp_ref[...]
    vwin[ROW_BLOCK:2 * ROW_BLOCK, :] = vc_ref[...]
    first_block = pl.program_id(1) == 0
    lane = lax.broadcasted_iota(jnp.int32, (ATT_QBLOCK, LANES), 1)
    low_half = lane < ATT_HEAD_DIM
    scale = jnp.bfloat16(ATT_HEAD_DIM ** -0.5)

    def scores(j):
        slot = j % 2
        for hp in range(ATT_HEADS // 2):
            cols = slice(hp * LANES, (hp + 1) * LANES)
            qp = q_ref[j * ATT_QBLOCK:(j + 1) * ATT_QBLOCK, cols] * scale
            kw = kwin[j * ATT_QBLOCK:j * ATT_QBLOCK + ATT_WINDOW, cols]
            for half in range(2):
                h = 2 * hp + half
                keep = low_half if half == 0 else jnp.logical_not(low_half)
                qh = jnp.where(keep, qp, jnp.zeros_like(qp))
                s_scr[slot, h] = lax.dot_general(qh, kw, (((1,), (1,)), ((), ())),
                                                 preferred_element_type=jnp.float32)

    def softmax(j):
        slot = j % 2
        variant = jnp.where(first_block, j, N_SUB)
        for h in range(ATT_HEADS):
            for g in range(ATT_QBLOCK // SOFTMAX_ROWS):
                rows = slice(g * SOFTMAX_ROWS, (g + 1) * SOFTMAX_ROWS)
                s = s_scr[slot, h, rows, :] + bias_scr[variant, h, rows, :]
                e = jnp.exp(s - jnp.max(s, axis=-1, keepdims=True))
                e_scr[slot, h, rows, :] = e.astype(jnp.bfloat16)
                l_scr[slot, h, rows, :] = jnp.broadcast_to(
                    jnp.sum(e, axis=-1, keepdims=True), (SOFTMAX_ROWS, LANES))

    def values(j):
        slot = j % 2
        for hp in range(ATT_HEADS // 2):
            cols = slice(hp * LANES, (hp + 1) * LANES)
            vw = vwin[j * ATT_QBLOCK:j * ATT_QBLOCK + ATT_WINDOW, cols]
            o_lo = jnp.dot(e_scr[slot, 2 * hp], vw, preferred_element_type=jnp.float32)
            o_hi = jnp.dot(e_scr[slot, 2 * hp + 1], vw, preferred_element_type=jnp.float32)
            denom = jnp.where(low_half, l_scr[slot, 2 * hp], l_scr[slot, 2 * hp + 1])
            o = jnp.where(low_half, o_lo, o_hi) * (1.0 / denom)
            o_ref[j * ATT_QBLOCK:(j + 1) * ATT_QBLOCK, cols] = o.astype(o_ref.dtype)

    scores(0)
    for j in range(N_SUB):
        if j + 1 < N_SUB:
            scores(j + 1)
        softmax(j)
        values(j)


def _attention(qkv, ext, batch, seq):
    t = qkv.shape[0]
    nb = seq // ROW_BLOCK
    cur = lambda col: (lambda b, i: (b * nb + i, col))
    prev = lambda col: (lambda b, i: (b * nb + jnp.maximum(i - 1, 0), col))
    blk = (ROW_BLOCK, ATT_WIDTH)
    return pl.pallas_call(
        _attn_kernel,
        grid=(batch, nb),
        in_specs=[pl.BlockSpec(blk, cur(0)),
                  pl.BlockSpec(blk, prev(1)), pl.BlockSpec(blk, cur(1)),
                  pl.BlockSpec(blk, prev(2)), pl.BlockSpec(blk, cur(2)),
                  _resident((ATT_HEADS, 1, EXT_WIDTH))],
        out_specs=pl.BlockSpec(blk, cur(0)),
        out_shape=jax.ShapeDtypeStruct((t, ATT_WIDTH), jnp.bfloat16),
        scratch_shapes=[pltpu.VMEM((2 * ROW_BLOCK, ATT_WIDTH), jnp.bfloat16),
                        pltpu.VMEM((2 * ROW_BLOCK, ATT_WIDTH), jnp.bfloat16),
                        pltpu.VMEM((N_SUB + 1, ATT_HEADS, ATT_QBLOCK, ATT_WINDOW), jnp.float32),
                        pltpu.VMEM((2, ATT_HEADS, ATT_QBLOCK, ATT_WINDOW), jnp.float32),
                        pltpu.VMEM((2, ATT_HEADS, ATT_QBLOCK, ATT_WINDOW), jnp.bfloat16),
                        pltpu.VMEM((2, ATT_HEADS, ATT_QBLOCK, LANES), jnp.float32)],
        compiler_params=_params(2),
        name="attention",
    )(qkv, qkv, qkv, qkv, qkv, ext)


def _extended_rel_bias(rel_bias):
    d_first = PAST + ATT_QBLOCK - 1
    d_last = d_first - (EXT_WIDTH - 1)
    assert d_first >= MAX_REL_DIST and d_last == -MAX_REL_DIST
    n_clipped = d_first - MAX_REL_DIST
    table = rel_bias.astype(jnp.float32)
    top = jnp.broadcast_to(table[:, 2 * MAX_REL_DIST:], (ATT_HEADS, n_clipped))
    return jnp.concatenate([top, table[:, ::-1]], axis=1)[:, None, :]


def _sgu_kernel(u_ref, v_ref, lng_ref, lnb_ref, w_ref, b_ref, o_ref):
    t_to = lax.broadcasted_iota(jnp.int32, (SGU_CHUNK, SGU_CHUNK), 0)
    t_from = lax.broadcasted_iota(jnp.int32, (SGU_CHUNK, SGU_CHUNK), 1)
    causal = t_to >= t_from
    ws = [jnp.where(causal, w_ref[g], 0.0).astype(jnp.bfloat16) for g in range(SGU_GROUPS)]
    for c in range(ROW_BLOCK // SGU_CHUNK):
        rows = slice(c * SGU_CHUNK, (c + 1) * SGU_CHUNK)
        gu = _gelu_tanh(u_ref[rows, :].astype(jnp.float32))
        gv = _gelu_tanh(v_ref[rows, :].astype(jnp.float32))
        mu = jnp.mean(gv, axis=-1, keepdims=True)
        d = gv - mu
        var = jnp.mean(d * d, axis=-1, keepdims=True)
        vn = (d * lax.rsqrt(var + EPS) * lng_ref[...] + lnb_ref[...]).astype(jnp.bfloat16)
        for g in range(SGU_GROUPS):
            cols = slice(g * SGU_GROUP_DIM, (g + 1) * SGU_GROUP_DIM)
            mixed = jnp.dot(ws[g], vn[:, cols], preferred_element_type=jnp.float32) + b_ref[:, cols]
            o_ref[rows, cols] = (gu[:, cols] * mixed).astype(o_ref.dtype)


def _sgu(uv, ln_g, ln_b, w, b_full):
    t = uv.shape[0]
    blk = (ROW_BLOCK, SGU_WIDTH)
    return pl.pallas_call(
        _sgu_kernel,
        grid=(t // ROW_BLOCK,),
        in_specs=[pl.BlockSpec(blk, lambda i: (i, 0)),
                  pl.BlockSpec(blk, lambda i: (i, 1)),
                  _resident((1, SGU_WIDTH)), _resident((1, SGU_WIDTH)),
                  _resident((SGU_GROUPS, SGU_CHUNK, SGU_CHUNK)),
                  _resident((SGU_CHUNK, SGU_WIDTH))],
        out_specs=pl.BlockSpec(blk, lambda i: (i, 0)),
        out_shape=jax.ShapeDtypeStruct((t, SGU_WIDTH), jnp.bfloat16),
        compiler_params=_params(1),
        name="sgu",
    )(uv, uv, ln_g, ln_b, w, b_full)


def _merge_kernel(att_ref, sgu_ref, gate_ref, h_ref, wa_ref, ws_ref, bg_ref, wo_ref, o_ref):
    br_att = jnp.dot(att_ref[...], wa_ref[...], preferred_element_type=jnp.float32)
    br_sgu = jnp.dot(sgu_ref[...], ws_ref[...], preferred_element_type=jnp.float32)
    g_att = jax.nn.sigmoid(gate_ref[:, 0:D_MODEL].astype(jnp.float32) + bg_ref[0:1, :])
    g_sgu = jax.nn.sigmoid(gate_ref[:, D_MODEL:GATE_WIDTH].astype(jnp.float32) + bg_ref[1:2, :])
    merged = (g_att * br_att + g_sgu * br_sgu).astype(jnp.bfloat16)
    o_ref[...] = h_ref[...] + jnp.dot(merged, wo_ref[...], preferred_element_type=jnp.float32)


def _merge(att, sgu, gate, h, wa, ws, bg, wo):
    t = h.shape[0]
    row = lambda i: (i, 0)
    return pl.pallas_call(
        _merge_kernel,
        grid=(t // ROW_BLOCK,),
        in_specs=[pl.BlockSpec((ROW_BLOCK, ATT_WIDTH), row),
                  pl.BlockSpec((ROW_BLOCK, SGU_WIDTH), row),
                  pl.BlockSpec((ROW_BLOCK, GATE_WIDTH), row),
                  pl.BlockSpec((ROW_BLOCK, D_MODEL), row),
                  _resident((ATT_WIDTH, D_MODEL)), _resident((SGU_WIDTH, D_MODEL)),
                  _resident((N_BRANCHES, D_MODEL)), _resident((D_MODEL, D_MODEL))],
        out_specs=pl.BlockSpec((ROW_BLOCK, D_MODEL), row),
        out_shape=jax.ShapeDtypeStruct((t, D_MODEL), jnp.float32),
        compiler_params=_params(1),
        name="merge",
    )(att, sgu, gate, h, wa, ws, bg, wo)


def _ffn_kernel(h_ref, g_ref, wi_ref, wo_ref, gf_ref, o_ref, *, final_norm):
    h = h_ref[...]
    hn = _rmsnorm_f32(h, g_ref[...]).astype(jnp.bfloat16)
    gate = jnp.dot(hn, wi_ref[:, 0:D_FF], preferred_element_type=jnp.float32)
    up = jnp.dot(hn, wi_ref[:, D_FF:2 * D_FF], preferred_element_type=jnp.float32)
    act = (gate * jax.nn.sigmoid(gate) * up).astype(jnp.bfloat16)
    out = h + jnp.dot(act, wo_ref[...], preferred_element_type=jnp.float32)
    if final_norm:
        out = _rmsnorm_f32(out, gf_ref[...])
    o_ref[...] = out


def _ffn(h, g, wi, wo, g_final, final_norm):
    t = h.shape[0]
    row = lambda i: (i, 0)
    return pl.pallas_call(
        functools.partial(_ffn_kernel, final_norm=final_norm),
        grid=(t // ROW_BLOCK,),
        in_specs=[pl.BlockSpec((ROW_BLOCK, D_MODEL), row),
                  _resident((1, D_MODEL)),
                  _resident((D_MODEL, 2 * D_FF)), _resident((D_FF, D_MODEL)),
                  _resident((1, D_MODEL))],
        out_specs=pl.BlockSpec((ROW_BLOCK, D_MODEL), row),
        out_shape=jax.ShapeDtypeStruct((t, D_MODEL), jnp.float32),
        compiler_params=_params(1),
        name="ffn_final" if final_norm else "ffn",
    )(h, g, wi, wo, g_final)


def kernel(x, norm_mix, w_in, att_rel_bias, sgu_norm_gain, sgu_norm_bias, sgu_w, sgu_b, w_br_att, w_br_sgu, b_gate, w_out, norm_ffn, w_ffn_in, w_ffn_out, norm_final):
    batch, seq, d = x.shape
    depth = w_in.shape[0]
    assert d == D_MODEL and seq % ROW_BLOCK == 0
    bf16 = jnp.bfloat16
    h = x.reshape(batch * seq, d)
    for l in range(depth):
        qkv, uv, gate = _proj(h, norm_mix[l][None], w_in[l].astype(bf16))
        att = _attention(qkv, _extended_rel_bias(att_rel_bias[l]), batch, seq)
        b_full = jnp.repeat(sgu_b[l].T, SGU_GROUP_DIM, axis=1)
        sgu = _sgu(uv, sgu_norm_gain[l][None], sgu_norm_bias[l][None], sgu_w[l], b_full)
        h = _merge(att, sgu, gate, h, w_br_att[l].astype(bf16), w_br_sgu[l].astype(bf16),
                   b_gate[l], w_out[l].astype(bf16))
        h = _ffn(h, norm_ffn[l][None], w_ffn_in[l].astype(bf16), w_ffn_out[l].astype(bf16),
                 norm_final[None], final_norm=(l == depth - 1))
    return h.reshape(batch, seq, d)
```

```python
import functools
import math

import jax
import jax.numpy as jnp
from jax import lax
from jax.experimental import pallas as pl
from jax.experimental.pallas import tpu as pltpu

D_MODEL = 1024
CHUNK = 64
LEFT_CHUNKS = 8
ATT_HEADS = 8
ATT_HEAD_DIM = 64
ATT_WIDTH = ATT_HEADS * ATT_HEAD_DIM
MAX_REL_DIST = 128
SGU_CHUNK = 128
SGU_GROUPS = 4
SGU_WIDTH = 512
SGU_GROUP_DIM = SGU_WIDTH // SGU_GROUPS
N_BRANCHES = 2
GATE_WIDTH = N_BRANCHES * D_MODEL
QKV_WIDTH = 3 * ATT_WIDTH
UV_WIDTH = 2 * SGU_WIDTH
IN_WIDTH = QKV_WIDTH + UV_WIDTH + GATE_WIDTH
D_FF = 2816
EPS = 1e-6
MASK_VALUE = -1e30

V7X_VMEM_BYTES = 64 * 1024 * 1024
VMEM_LIMIT_BYTES = V7X_VMEM_BYTES - 8 * 1024 * 1024
LANES = 128

ROW_BLOCK = 512
ATT_QBLOCK = 2 * CHUNK
ATT_WINDOW = (LEFT_CHUNKS + 2) * CHUNK
PAST = LEFT_CHUNKS * CHUNK
EXT_WIDTH = ATT_WINDOW + ATT_QBLOCK


def _params(n_axes, flags=None):
    return pltpu.CompilerParams(
        dimension_semantics=("arbitrary",) * n_axes,
        vmem_limit_bytes=VMEM_LIMIT_BYTES,
        flags=flags)


def _rmsnorm_f32(x, g):
    return x * lax.rsqrt(jnp.mean(x * x, axis=-1, keepdims=True) + EPS) * g


def _gelu_tanh(x):
    c = math.sqrt(2.0 / math.pi)
    return 0.5 * x * (1.0 + jnp.tanh(c * (x + 0.044715 * (x * x * x))))


def _resident(shape):
    return pl.BlockSpec(shape, lambda *_: (0,) * len(shape), pipeline_mode=pl.Buffered(1))


def _layer_slice(layer, shape):
    return pl.BlockSpec((None,) + shape, lambda *_: (layer,) + (0,) * len(shape),
                        pipeline_mode=pl.Buffered(1))


PROJ_COLS = 512


def _proj_kernel(x_ref, g_ref, w_ref, lng_ref, lnb_ref, sw_ref, sb_ref, bg_ref,
                 qkv_ref, sgu_ref, gate_ref):
    xn = _rmsnorm_f32(x_ref[...], g_ref[...]).astype(jnp.bfloat16)

    def project(start):
        return jnp.dot(xn, w_ref[:, start:start + PROJ_COLS], preferred_element_type=jnp.float32)

    gu = _gelu_tanh(project(QKV_WIDTH))
    gv = _gelu_tanh(project(QKV_WIDTH + SGU_WIDTH))
    mu = jnp.mean(gv, axis=-1, keepdims=True)
    dv = gv - mu
    var = jnp.mean(dv * dv, axis=-1, keepdims=True)
    vn = (dv * lax.rsqrt(var + EPS) * lng_ref[...] + lnb_ref[...]).astype(jnp.bfloat16)

    for c in range(0, GATE_WIDTH, PROJ_COLS):
        logits = project(QKV_WIDTH + UV_WIDTH + c) + bg_ref[:, c:c + PROJ_COLS]
        gate_ref[:, c:c + PROJ_COLS] = jax.nn.sigmoid(logits).astype(gate_ref.dtype)
    for c in range(0, QKV_WIDTH, PROJ_COLS):
        qkv_ref[:, c:c + PROJ_COLS] = project(c).astype(qkv_ref.dtype)

    t_to = lax.broadcasted_iota(jnp.int32, (SGU_CHUNK, SGU_CHUNK), 0)
    t_from = lax.broadcasted_iota(jnp.int32, (SGU_CHUNK, SGU_CHUNK), 1)
    for g in range(SGU_GROUPS):
        cols = slice(g * SGU_GROUP_DIM, (g + 1) * SGU_GROUP_DIM)
        wg = jnp.where(t_to >= t_from, sw_ref[g], 0.0).astype(jnp.bfloat16)
        for c in range(ROW_BLOCK // SGU_CHUNK):
            rows = slice(c * SGU_CHUNK, (c + 1) * SGU_CHUNK)
            mixed = jnp.dot(wg, vn[rows, cols], preferred_element_type=jnp.float32) + sb_ref[:, cols]
            sgu_ref[rows, cols] = (gu[rows, cols] * mixed).astype(sgu_ref.dtype)


def _proj(layer, h, g, w, ln_g, ln_b, sgu_w, sgu_b_full, b_gate):
    t = h.shape[0]
    row = lambda i: (i, 0)
    return pl.pallas_call(
        _proj_kernel,
        grid=(t // ROW_BLOCK,),
        in_specs=[pl.BlockSpec((ROW_BLOCK, D_MODEL), row),
                  _resident((1, D_MODEL)),
                  _layer_slice(layer, (D_MODEL, IN_WIDTH)),
                  _resident((1, SGU_WIDTH)), _resident((1, SGU_WIDTH)),
                  _layer_slice(layer, (SGU_GROUPS, SGU_CHUNK, SGU_CHUNK)),
                  _resident((SGU_CHUNK, SGU_WIDTH)),
                  _resident((1, GATE_WIDTH))],
        out_specs=[pl.BlockSpec((ROW_BLOCK, QKV_WIDTH), row),
                   pl.BlockSpec((ROW_BLOCK, SGU_WIDTH), row),
                   pl.BlockSpec((ROW_BLOCK, GATE_WIDTH), row)],
        out_shape=[jax.ShapeDtypeStruct((t, QKV_WIDTH), jnp.bfloat16),
                   jax.ShapeDtypeStruct((t, SGU_WIDTH), jnp.bfloat16),
                   jax.ShapeDtypeStruct((t, GATE_WIDTH), jnp.bfloat16)],
        compiler_params=_params(1),
        name="proj",
    )(h, g, w, ln_g, ln_b, sgu_w, sgu_b_full, b_gate)


SOFTMAX_ROWS = 16
N_SUB = ROW_BLOCK // ATT_QBLOCK


def _attn_kernel(q_ref, kp_ref, kc_ref, vp_ref, vc_ref, ext_ref, o_ref,
                 kwin, vwin, bias_scr, s_scr, e_scr, l_scr):
    @pl.when((pl.program_id(0) == 0) & (pl.program_id(1) == 0))
    def _expand_bias():
        r = lax.broadcasted_iota(jnp.int32, (ATT_QBLOCK, ATT_WINDOW), 0)
        c = lax.broadcasted_iota(jnp.int32, (ATT_QBLOCK, ATT_WINDOW), 1)
        qc, kc = r // CHUNK, c // CHUNK
        visible = (kc >= qc) & (kc <= qc + LEFT_CHUNKS)
        for h in range(ATT_HEADS):
            rows = jnp.broadcast_to(ext_ref[h], (ATT_QBLOCK, EXT_WIDTH))
            toeplitz = pltpu.roll(rows, EXT_WIDTH - (ATT_QBLOCK - 1), 1, stride=1, stride_axis=0)
            base = jnp.where(visible, toeplitz[:, 0:ATT_WINDOW], MASK_VALUE)
            bias_scr[N_SUB, h] = base
            for j in range(N_SUB):
                bias_scr[j, h] = jnp.where(c < PAST - j * ATT_QBLOCK, MASK_VALUE, base)

    kwin[0:ROW_BLOCK, :] = kp_ref[...]
    kwin[ROW_BLOCK:2 * ROW_BLOCK, :] = kc_ref[...]
    vwin[0:ROW_BLOCK, :] = vp_ref[...]
    vwin[ROW_BLOCK:2 * ROW_BLOCK, :] = vc_ref[...]
    first_block = pl.program_id(1) == 0
    lane = lax.broadcasted_iota(jnp.int32, (ATT_QBLOCK, LANES), 1)
    low_half = lane < ATT_HEAD_DIM
    scale = jnp.bfloat16(ATT_HEAD_DIM ** -0.5)

    def scores(j):
        slot = j % 2
        for hp in range(ATT_HEADS // 2):
            cols = slice(hp * LANES, (hp + 1) * LANES)
            qp = q_ref[j * ATT_QBLOCK:(j + 1) * ATT_QBLOCK, cols] * scale
            kw = kwin[j * ATT_QBLOCK:j * ATT_QBLOCK + ATT_WINDOW, cols]
            for half in range(2):
                h = 2 * hp + half
                keep = low_half if half == 0 else jnp.logical_not(low_half)
                qh = jnp.where(keep, qp, jnp.zeros_like(qp))
                s_scr[slot, h] = lax.dot_general(qh, kw, (((1,), (1,)), ((), ())),
                                                 preferred_element_type=jnp.float32)

    def softmax(j):
        slot = j % 2
        variant = jnp.where(first_block, j, N_SUB)
        for h in range(ATT_HEADS):
            for g in range(ATT_QBLOCK // SOFTMAX_ROWS):
                rows = slice(g * SOFTMAX_ROWS, (g + 1) * SOFTMAX_ROWS)
                s = s_scr[slot, h, rows, :] + bias_scr[variant, h, rows, :]
                e = jnp.exp(s - jnp.max(s, axis=-1, keepdims=True))
                e_scr[slot, h, rows, :] = e.astype(jnp.bfloat16)
                l_scr[slot, h, rows, :] = jnp.broadcast_to(
                    jnp.sum(e, axis=-1, keepdims=True), (SOFTMAX_ROWS, LANES))

    def values(j):
        slot = j % 2
        for hp in range(ATT_HEADS // 2):
            cols = slice(hp * LANES, (hp + 1) * LANES)
            vw = vwin[j * ATT_QBLOCK:j * ATT_QBLOCK + ATT_WINDOW, cols]
            o_lo = jnp.dot(e_scr[slot, 2 * hp], vw, preferred_element_type=jnp.float32)
            o_hi = jnp.dot(e_scr[slot, 2 * hp + 1], vw, preferred_element_type=jnp.float32)
            denom = jnp.where(low_half, l_scr[slot, 2 * hp], l_scr[slot, 2 * hp + 1])
            o = jnp.where(low_half, o_lo, o_hi) * (1.0 / denom)
            o_ref[j * ATT_QBLOCK:(j + 1) * ATT_QBLOCK, cols] = o.astype(o_ref.dtype)

    scores(0)
    for j in range(N_SUB):
        if j + 1 < N_SUB:
            scores(j + 1)
        softmax(j)
        values(j)


def _attention(qkv, ext, batch, seq):
    t = qkv.shape[0]
    nb = seq // ROW_BLOCK
    cur = lambda col: (lambda b, i: (b * nb + i, col))
    prev = lambda col: (lambda b, i: (b * nb + jnp.maximum(i - 1, 0), col))
    blk = (ROW_BLOCK, ATT_WIDTH)
    return pl.pallas_call(
        _attn_kernel,
        grid=(batch, nb),
        in_specs=[pl.BlockSpec(blk, cur(0)),
                  pl.BlockSpec(blk, prev(1)), pl.BlockSpec(blk, cur(1)),
                  pl.BlockSpec(blk, prev(2)), pl.BlockSpec(blk, cur(2)),
                  _resident((ATT_HEADS, 1, EXT_WIDTH))],
        out_specs=pl.BlockSpec(blk, cur(0)),
        out_shape=jax.ShapeDtypeStruct((t, ATT_WIDTH), jnp.bfloat16),
        scratch_shapes=[pltpu.VMEM((2 * ROW_BLOCK, ATT_WIDTH), jnp.bfloat16),
                        pltpu.VMEM((2 * ROW_BLOCK, ATT_WIDTH), jnp.bfloat16),
                        pltpu.VMEM((N_SUB + 1, ATT_HEADS, ATT_QBLOCK, ATT_WINDOW), jnp.float32),
                        pltpu.VMEM((2, ATT_HEADS, ATT_QBLOCK, ATT_WINDOW), jnp.float32),
                        pltpu.VMEM((2, ATT_HEADS, ATT_QBLOCK, ATT_WINDOW), jnp.bfloat16),
                        pltpu.VMEM((2, ATT_HEADS, ATT_QBLOCK, LANES), jnp.float32)],
        compiler_params=_params(2),
        name="attention",
    )(qkv, qkv, qkv, qkv, qkv, ext)


def _extended_rel_bias(rel_bias):
    d_first = PAST + ATT_QBLOCK - 1
    d_last = d_first - (EXT_WIDTH - 1)
    assert d_first >= MAX_REL_DIST and d_last == -MAX_REL_DIST
    n_clipped = d_first - MAX_REL_DIST
    table = rel_bias.astype(jnp.float32)
    top = jnp.broadcast_to(table[:, 2 * MAX_REL_DIST:], (ATT_HEADS, n_clipped))
    return jnp.concatenate([top, table[:, ::-1]], axis=1)[:, None, :]


def _merge_kernel(att_ref, sgu_ref, gate_ref, h_ref, wa_ref, ws_ref, wo_ref, o_ref):
    br_att = jnp.dot(att_ref[...], wa_ref[...], preferred_element_type=jnp.float32)
    br_sgu = jnp.dot(sgu_ref[...], ws_ref[...], preferred_element_type=jnp.float32)
    g_att = gate_ref[:, 0:D_MODEL].astype(jnp.float32)
    g_sgu = gate_ref[:, D_MODEL:GATE_WIDTH].astype(jnp.float32)
    merged = (g_att * br_att + g_sgu * br_sgu).astype(jnp.bfloat16)
    o_ref[...] = h_ref[...] + jnp.dot(merged, wo_ref[...], preferred_element_type=jnp.float32)


def _merge(layer, att, sgu, gate, h, wa, ws, wo):
    t = h.shape[0]
    row = lambda i: (i, 0)
    return pl.pallas_call(
        _merge_kernel,
        grid=(t // ROW_BLOCK,),
        in_specs=[pl.BlockSpec((ROW_BLOCK, ATT_WIDTH), row),
                  pl.BlockSpec((ROW_BLOCK, SGU_WIDTH), row),
                  pl.BlockSpec((ROW_BLOCK, GATE_WIDTH), row),
                  pl.BlockSpec((ROW_BLOCK, D_MODEL), row),
                  _layer_slice(layer, (ATT_WIDTH, D_MODEL)),
                  _layer_slice(layer, (SGU_WIDTH, D_MODEL)),
                  _layer_slice(layer, (D_MODEL, D_MODEL))],
        out_specs=pl.BlockSpec((ROW_BLOCK, D_MODEL), row),
        out_shape=jax.ShapeDtypeStruct((t, D_MODEL), jnp.float32),
        compiler_params=_params(1),
        name="merge",
    )(att, sgu, gate, h, wa, ws, wo)


def _ffn_kernel(h_ref, g_ref, wi_ref, wo_ref, gf_ref, o_ref, *, final_norm):
    h = h_ref[...]
    hn = _rmsnorm_f32(h, g_ref[...]).astype(jnp.bfloat16)
    gate = jnp.dot(hn, wi_ref[:, 0:D_FF], preferred_element_type=jnp.float32)
    up = jnp.dot(hn, wi_ref[:, D_FF:2 * D_FF], preferred_element_type=jnp.float32)
    act = (gate * jax.nn.sigmoid(gate) * up).astype(jnp.bfloat16)
    out = h + jnp.dot(act, wo_ref[...], preferred_element_type=jnp.float32)
    if final_norm:
        out = _rmsnorm_f32(out, gf_ref[...])
    o_ref[...] = out


def _ffn(layer, h, g, wi, wo, g_final, final_norm):
    t = h.shape[0]
    row = lambda i: (i, 0)
    return pl.pallas_call(
        functools.partial(_ffn_kernel, final_norm=final_norm),
        grid=(t // ROW_BLOCK,),
        in_specs=[pl.BlockSpec((ROW_BLOCK, D_MODEL), row),
                  _resident((1, D_MODEL)),
                  _layer_slice(layer, (D_MODEL, 2 * D_FF)),
                  _layer_slice(layer, (D_FF, D_MODEL)),
                  _resident((1, D_MODEL))],
        out_specs=pl.BlockSpec((ROW_BLOCK, D_MODEL), row),
        out_shape=jax.ShapeDtypeStruct((t, D_MODEL), jnp.float32),
        compiler_params=_params(1),
        name="ffn_final" if final_norm else "ffn",
    )(h, g, wi, wo, g_final)


def kernel(x, norm_mix, w_in, att_rel_bias, sgu_norm_gain, sgu_norm_bias, sgu_w, sgu_b, w_br_att, w_br_sgu, b_gate, w_out, norm_ffn, w_ffn_in, w_ffn_out, norm_final):
    batch, seq, d = x.shape
    depth = w_in.shape[0]
    assert d == D_MODEL and seq % ROW_BLOCK == 0
    bf16 = jnp.bfloat16
    w_in, w_br_att, w_br_sgu, w_out, w_ffn_in, w_ffn_out = (
        w.astype(bf16) for w in (w_in, w_br_att, w_br_sgu, w_out, w_ffn_in, w_ffn_out))
    h = x.reshape(batch * seq, d)
    for l in range(depth):
        sgu_b_full = jnp.repeat(sgu_b[l].T, SGU_GROUP_DIM, axis=1)
        qkv, sgu, gate = _proj(l, h, norm_mix[l][None], w_in, sgu_norm_gain[l][None],
                               sgu_norm_bias[l][None], sgu_w, sgu_b_full,
                               b_gate[l].reshape(1, GATE_WIDTH))
        att = _attention(qkv, _extended_rel_bias(att_rel_bias[l]), batch, seq)
        h = _merge(l, att, sgu, gate, h, w_br_att, w_br_sgu, w_out)
        h = _ffn(l, h, norm_ffn[l][None], w_ffn_in, w_ffn_out, norm_final[None],
                 final_norm=(l == depth - 1))
    return h.reshape(batch, seq, d)
```

```python
import functools
import math

import jax
import jax.numpy as jnp
from jax import lax
from jax.experimental import pallas as pl
from jax.experimental.pallas import tpu as pltpu

D_MODEL = 1024
CHUNK = 64
LEFT_CHUNKS = 8
ATT_HEADS = 8
ATT_HEAD_DIM = 64
ATT_WIDTH = ATT_HEADS * ATT_HEAD_DIM
MAX_REL_DIST = 128
SGU_CHUNK = 128
SGU_GROUPS = 4
SGU_WIDTH = 512
SGU_GROUP_DIM = SGU_WIDTH // SGU_GROUPS
N_BRANCHES = 2
GATE_WIDTH = N_BRANCHES * D_MODEL
QKV_WIDTH = 3 * ATT_WIDTH
UV_WIDTH = 2 * SGU_WIDTH
IN_WIDTH = QKV_WIDTH + UV_WIDTH + GATE_WIDTH
D_FF = 2816
EPS = 1e-6
MASK_VALUE = -1e30

V7X_VMEM_BYTES = 64 * 1024 * 1024
VMEM_LIMIT_BYTES = V7X_VMEM_BYTES - 8 * 1024 * 1024
LANES = 128

ROW_BLOCK = 512
ATT_QBLOCK = 2 * CHUNK
ATT_WINDOW = (LEFT_CHUNKS + 2) * CHUNK
PAST = LEFT_CHUNKS * CHUNK
EXT_WIDTH = ATT_WINDOW + ATT_QBLOCK


def _params(n_axes, flags=None):
    return pltpu.CompilerParams(
        dimension_semantics=("arbitrary",) * n_axes,
        vmem_limit_bytes=VMEM_LIMIT_BYTES,
        flags=flags)


def _rmsnorm_f32(x, g):
    return x * lax.rsqrt(jnp.mean(x * x, axis=-1, keepdims=True) + EPS) * g


def _gelu_tanh(x):
    c = math.sqrt(2.0 / math.pi)
    return 0.5 * x * (1.0 + jnp.tanh(c * (x + 0.044715 * (x * x * x))))


def _resident(shape):
    return pl.BlockSpec(shape, lambda *_: (0,) * len(shape), pipeline_mode=pl.Buffered(1))


def _layer_slice(layer, shape):
    return pl.BlockSpec((None,) + shape, lambda *_: (layer,) + (0,) * len(shape),
                        pipeline_mode=pl.Buffered(1))


PROJ_COLS = 512


def _proj_kernel(x_ref, g_ref, w_ref, lng_ref, lnb_ref, sw_ref, sb_ref, bg_ref,
                 qkv_ref, sgu_ref, gate_ref):
    xn = _rmsnorm_f32(x_ref[...], g_ref[...]).astype(jnp.bfloat16)

    def project(start):
        return jnp.dot(xn, w_ref[:, start:start + PROJ_COLS], preferred_element_type=jnp.float32)

    gu = _gelu_tanh(project(QKV_WIDTH))
    gv = _gelu_tanh(project(QKV_WIDTH + SGU_WIDTH))
    mu = jnp.mean(gv, axis=-1, keepdims=True)
    dv = gv - mu
    var = jnp.mean(dv * dv, axis=-1, keepdims=True)
    vn = (dv * lax.rsqrt(var + EPS) * lng_ref[...] + lnb_ref[...]).astype(jnp.bfloat16)

    for c in range(0, GATE_WIDTH, PROJ_COLS):
        logits = project(QKV_WIDTH + UV_WIDTH + c) + bg_ref[:, c:c + PROJ_COLS]
        gate_ref[:, c:c + PROJ_COLS] = jax.nn.sigmoid(logits).astype(gate_ref.dtype)
    for c in range(0, QKV_WIDTH, PROJ_COLS):
        qkv_ref[:, c:c + PROJ_COLS] = project(c).astype(qkv_ref.dtype)

    t_to = lax.broadcasted_iota(jnp.int32, (SGU_CHUNK, SGU_CHUNK), 0)
    t_from = lax.broadcasted_iota(jnp.int32, (SGU_CHUNK, SGU_CHUNK), 1)
    for g in range(SGU_GROUPS):
        cols = slice(g * SGU_GROUP_DIM, (g + 1) * SGU_GROUP_DIM)
        wg = jnp.where(t_to >= t_from, sw_ref[g], 0.0).astype(jnp.bfloat16)
        for c in range(ROW_BLOCK // SGU_CHUNK):
            rows = slice(c * SGU_CHUNK, (c + 1) * SGU_CHUNK)
            mixed = jnp.dot(wg, vn[rows, cols], preferred_element_type=jnp.float32) + sb_ref[:, cols]
            sgu_ref[rows, cols] = (gu[rows, cols] * mixed).astype(sgu_ref.dtype)


def _proj(layer, h, g, w, ln_g, ln_b, sgu_w, sgu_b_full, b_gate):
    t = h.shape[0]
    row = lambda i: (i, 0)
    return pl.pallas_call(
        _proj_kernel,
        grid=(t // ROW_BLOCK,),
        in_specs=[pl.BlockSpec((ROW_BLOCK, D_MODEL), row),
                  _resident((1, D_MODEL)),
                  _layer_slice(layer, (D_MODEL, IN_WIDTH)),
                  _resident((1, SGU_WIDTH)), _resident((1, SGU_WIDTH)),
                  _layer_slice(layer, (SGU_GROUPS, SGU_CHUNK, SGU_CHUNK)),
                  _resident((SGU_CHUNK, SGU_WIDTH)),
                  _resident((1, GATE_WIDTH))],
        out_specs=[pl.BlockSpec((ROW_BLOCK, QKV_WIDTH), row),
                   pl.BlockSpec((ROW_BLOCK, SGU_WIDTH), row),
                   pl.BlockSpec((ROW_BLOCK, GATE_WIDTH), row)],
        out_shape=[jax.ShapeDtypeStruct((t, QKV_WIDTH), jnp.bfloat16),
                   jax.ShapeDtypeStruct((t, SGU_WIDTH), jnp.bfloat16),
                   jax.ShapeDtypeStruct((t, GATE_WIDTH), jnp.bfloat16)],
        compiler_params=_params(1),
        name="proj",
    )(h, g, w, ln_g, ln_b, sgu_w, sgu_b_full, b_gate)


SOFTMAX_ROWS = 16
N_SUB = ROW_BLOCK // ATT_QBLOCK


def _attn_kernel(q_ref, kp_ref, kc_ref, vp_ref, vc_ref, ext_ref, o_ref,
                 kwin, vwin, bias_scr, s_scr, e_scr, l_scr):
    @pl.when((pl.program_id(0) == 0) & (pl.program_id(1) == 0))
    def _expand_bias():
        r = lax.broadcasted_iota(jnp.int32, (ATT_QBLOCK, ATT_WINDOW), 0)
        c = lax.broadcasted_iota(jnp.int32, (ATT_QBLOCK, ATT_WINDOW), 1)
        qc, kc = r // CHUNK, c // CHUNK
        visible = (kc >= qc) & (kc <= qc + LEFT_CHUNKS)
        for h in range(ATT_HEADS):
            rows = jnp.broadcast_to(ext_ref[h], (ATT_QBLOCK, EXT_WIDTH))
            toeplitz = pltpu.roll(rows, EXT_WIDTH - (ATT_QBLOCK - 1), 1, stride=1, stride_axis=0)
            base = jnp.where(visible, toeplitz[:, 0:ATT_WINDOW], MASK_VALUE)
            bias_scr[N_SUB, h] = base
            for j in range(N_SUB):
                bias_scr[j, h] = jnp.where(c < PAST - j * ATT_QBLOCK, MASK_VALUE, base)

    kwin[0:ROW_BLOCK, :] = kp_ref[...]
    kwin[ROW_BLOCK:2 * ROW_BLOCK, :] = kc_ref[...]
    vwin[0:ROW_BLOCK, :] = vp_ref[...]
    vwin[ROW_BLOCK:2 * ROW_BLOCK, :] = vc_ref[...]
    first_block = pl.program_id(1) == 0
    lane = lax.broadcasted_iota(jnp.int32, (ATT_QBLOCK, LANES), 1)
    low_half = lane < ATT_HEAD_DIM
    scale = jnp.bfloat16(ATT_HEAD_DIM ** -0.5)

    def scores(j):
        slot = j % 2
        for hp in range(ATT_HEADS // 2):
            cols = slice(hp * LANES, (hp + 1) * LANES)
            qp = q_ref[j * ATT_QBLOCK:(j + 1) * ATT_QBLOCK, cols] * scale
            kw = kwin[j * ATT_QBLOCK:j * ATT_QBLOCK + ATT_WINDOW, cols]
            for half in range(2):
                h = 2 * hp + half
                keep = low_half if half == 0 else jnp.logical_not(low_half)
                qh = jnp.where(keep, qp, jnp.zeros_like(qp))
                s_scr[slot, h] = lax.dot_general(qh, kw, (((1,), (1,)), ((), ())),
                                                 preferred_element_type=jnp.float32)

    def softmax(j):
        slot = j % 2
        variant = jnp.where(first_block, j, N_SUB)
        for h in range(ATT_HEADS):
            for g in range(ATT_QBLOCK // SOFTMAX_ROWS):
                rows = slice(g * SOFTMAX_ROWS, (g + 1) * SOFTMAX_ROWS)
                s = s_scr[slot, h, rows, :] + bias_scr[variant, h, rows, :]
                e = jnp.exp(s - jnp.max(s, axis=-1, keepdims=True))
                e_scr[slot, h, rows, :] = e.astype(jnp.bfloat16)
                l_scr[slot, h, rows, :] = jnp.broadcast_to(
                    jnp.sum(e, axis=-1, keepdims=True), (SOFTMAX_ROWS, LANES))

    def values(j):
        slot = j % 2
        for hp in range(ATT_HEADS // 2):
            cols = slice(hp * LANES, (hp + 1) * LANES)
            vw = vwin[j * ATT_QBLOCK:j * ATT_QBLOCK + ATT_WINDOW, cols]
            o_lo = jnp.dot(e_scr[slot, 2 * hp], vw, preferred_element_type=jnp.float32)
            o_hi = jnp.dot(e_scr[slot, 2 * hp + 1], vw, preferred_element_type=jnp.float32)
            denom = jnp.where(low_half, l_scr[slot, 2 * hp], l_scr[slot, 2 * hp + 1])
            o = jnp.where(low_half, o_lo, o_hi) * (1.0 / denom)
            o_ref[j * ATT_QBLOCK:(j + 1) * ATT_QBLOCK, cols] = o.astype(o_ref.dtype)

    scores(0)
    for j in range(N_SUB):
        if j + 1 < N_SUB:
            scores(j + 1)
        softmax(j)
        values(j)


def _attention(qkv, ext, batch, seq):
    t = qkv.shape[0]
    nb = seq // ROW_BLOCK
    cur = lambda col: (lambda b, i: (b * nb + i, col))
    prev = lambda col: (lambda b, i: (b * nb + jnp.maximum(i - 1, 0), col))
    blk = (ROW_BLOCK, ATT_WIDTH)
    return pl.pallas_call(
        _attn_kernel,
        grid=(batch, nb),
        in_specs=[pl.BlockSpec(blk, cur(0)),
                  pl.BlockSpec(blk, prev(1)), pl.BlockSpec(blk, cur(1)),
                  pl.BlockSpec(blk, prev(2)), pl.BlockSpec(blk, cur(2)),
                  _resident((ATT_HEADS, 1, EXT_WIDTH))],
        out_specs=pl.BlockSpec(blk, cur(0)),
        out_shape=jax.ShapeDtypeStruct((t, ATT_WIDTH), jnp.bfloat16),
        scratch_shapes=[pltpu.VMEM((2 * ROW_BLOCK, ATT_WIDTH), jnp.bfloat16),
                        pltpu.VMEM((2 * ROW_BLOCK, ATT_WIDTH), jnp.bfloat16),
                        pltpu.VMEM((N_SUB + 1, ATT_HEADS, ATT_QBLOCK, ATT_WINDOW), jnp.float32),
                        pltpu.VMEM((2, ATT_HEADS, ATT_QBLOCK, ATT_WINDOW), jnp.float32),
                        pltpu.VMEM((2, ATT_HEADS, ATT_QBLOCK, ATT_WINDOW), jnp.bfloat16),
                        pltpu.VMEM((2, ATT_HEADS, ATT_QBLOCK, LANES), jnp.float32)],
        compiler_params=_params(2),
        name="attention",
    )(qkv, qkv, qkv, qkv, qkv, ext)


def _extended_rel_bias(rel_bias):
    d_first = PAST + ATT_QBLOCK - 1
    d_last = d_first - (EXT_WIDTH - 1)
    assert d_first >= MAX_REL_DIST and d_last == -MAX_REL_DIST
    n_clipped = d_first - MAX_REL_DIST
    table = rel_bias.astype(jnp.float32)
    top = jnp.broadcast_to(table[:, 2 * MAX_REL_DIST:], (ATT_HEADS, n_clipped))
    return jnp.concatenate([top, table[:, ::-1]], axis=1)[:, None, :]


def _post_kernel(att_ref, sgu_ref, gate_ref, h_ref, wa_ref, ws_ref, wo_ref,
                 g_ref, wi_ref, wf_ref, gf_ref, o_ref, *, final_norm):
    br_att = jnp.dot(att_ref[...], wa_ref[...], preferred_element_type=jnp.float32)
    br_sgu = jnp.dot(sgu_ref[...], ws_ref[...], preferred_element_type=jnp.float32)
    g_att = gate_ref[:, 0:D_MODEL].astype(jnp.float32)
    g_sgu = gate_ref[:, D_MODEL:GATE_WIDTH].astype(jnp.float32)
    merged = (g_att * br_att + g_sgu * br_sgu).astype(jnp.bfloat16)
    h = h_ref[...] + jnp.dot(merged, wo_ref[...], preferred_element_type=jnp.float32)
    hn = _rmsnorm_f32(h, g_ref[...]).astype(jnp.bfloat16)
    gate = jnp.dot(hn, wi_ref[:, 0:D_FF], preferred_element_type=jnp.float32)
    up = jnp.dot(hn, wi_ref[:, D_FF:2 * D_FF], preferred_element_type=jnp.float32)
    act = (gate * jax.nn.sigmoid(gate) * up).astype(jnp.bfloat16)
    out = h + jnp.dot(act, wf_ref[...], preferred_element_type=jnp.float32)
    if final_norm:
        out = _rmsnorm_f32(out, gf_ref[...])
    o_ref[...] = out


def _post(layer, att, sgu, gate, h, wa, ws, wo, g, wi, wf, g_final, final_norm):
    t = h.shape[0]
    row = lambda i: (i, 0)
    return pl.pallas_call(
        functools.partial(_post_kernel, final_norm=final_norm),
        grid=(t // ROW_BLOCK,),
        in_specs=[pl.BlockSpec((ROW_BLOCK, ATT_WIDTH), row),
                  pl.BlockSpec((ROW_BLOCK, SGU_WIDTH), row),
                  pl.BlockSpec((ROW_BLOCK, GATE_WIDTH), row),
                  pl.BlockSpec((ROW_BLOCK, D_MODEL), row),
                  _layer_slice(layer, (ATT_WIDTH, D_MODEL)),
                  _layer_slice(layer, (SGU_WIDTH, D_MODEL)),
                  _layer_slice(layer, (D_MODEL, D_MODEL)),
                  _resident((1, D_MODEL)),
                  _layer_slice(layer, (D_MODEL, 2 * D_FF)),
                  _layer_slice(layer, (D_FF, D_MODEL)),
                  _resident((1, D_MODEL))],
        out_specs=pl.BlockSpec((ROW_BLOCK, D_MODEL), row),
        out_shape=jax.ShapeDtypeStruct((t, D_MODEL), jnp.float32),
        compiler_params=_params(1),
        name="post_final" if final_norm else "post",
    )(att, sgu, gate, h, wa, ws, wo, g, wi, wf, g_final)


def kernel(x, norm_mix, w_in, att_rel_bias, sgu_norm_gain, sgu_norm_bias, sgu_w, sgu_b, w_br_att, w_br_sgu, b_gate, w_out, norm_ffn, w_ffn_in, w_ffn_out, norm_final):
    batch, seq, d = x.shape
    depth = w_in.shape[0]
    assert d == D_MODEL and seq % ROW_BLOCK == 0
    bf16 = jnp.bfloat16
    w_in, w_br_att, w_br_sgu, w_out, w_ffn_in, w_ffn_out = (
        w.astype(bf16) for w in (w_in, w_br_att, w_br_sgu, w_out, w_ffn_in, w_ffn_out))
    h = x.reshape(batch * seq, d)
    for l in range(depth):
        sgu_b_full = jnp.repeat(sgu_b[l].T, SGU_GROUP_DIM, axis=1)
        qkv, sgu, gate = _proj(l, h, norm_mix[l][None], w_in, sgu_norm_gain[l][None],
                               sgu_norm_bias[l][None], sgu_w, sgu_b_full,
                               b_gate[l].reshape(1, GATE_WIDTH))
        att = _attention(qkv, _extended_rel_bias(att_rel_bias[l]), batch, seq)
        h = _post(l, att, sgu, gate, h, w_br_att, w_br_sgu, w_out, norm_ffn[l][None],
                  w_ffn_in, w_ffn_out, norm_final[None], final_norm=(l == depth - 1))
    return h.reshape(batch, seq, d)
```

```python
import functools
import math

import jax
import jax.numpy as jnp
from jax import lax
from jax.experimental import pallas as pl
from jax.experimental.pallas import tpu as pltpu

D_MODEL = 1024
CHUNK = 64
LEFT_CHUNKS = 8
ATT_HEADS = 8
ATT_HEAD_DIM = 64
ATT_WIDTH = ATT_HEADS * ATT_HEAD_DIM
MAX_REL_DIST = 128
SGU_CHUNK = 128
SGU_GROUPS = 4
SGU_WIDTH = 512
SGU_GROUP_DIM = SGU_WIDTH // SGU_GROUPS
N_BRANCHES = 2
GATE_WIDTH = N_BRANCHES * D_MODEL
QKV_WIDTH = 3 * ATT_WIDTH
UV_WIDTH = 2 * SGU_WIDTH
IN_WIDTH = QKV_WIDTH + UV_WIDTH + GATE_WIDTH
D_FF = 2816
EPS = 1e-6
MASK_VALUE = -1e30

V7X_VMEM_BYTES = 64 * 1024 * 1024
VMEM_LIMIT_BYTES = V7X_VMEM_BYTES - 8 * 1024 * 1024
LANES = 128

ROW_BLOCK = 512
ATT_QBLOCK = 2 * CHUNK
ATT_WINDOW = (LEFT_CHUNKS + 2) * CHUNK
PAST = LEFT_CHUNKS * CHUNK
EXT_WIDTH = ATT_WINDOW + ATT_QBLOCK


def _params(n_axes, flags=None):
    return pltpu.CompilerParams(
        dimension_semantics=("arbitrary",) * n_axes,
        vmem_limit_bytes=VMEM_LIMIT_BYTES,
        flags=flags)


def _rmsnorm_f32(x, g):
    return x * lax.rsqrt(jnp.mean(x * x, axis=-1, keepdims=True) + EPS) * g


def _gelu_tanh(x):
    c = math.sqrt(2.0 / math.pi)
    return 0.5 * x * (1.0 + jnp.tanh(c * (x + 0.044715 * (x * x * x))))


def _resident(shape):
    return pl.BlockSpec(shape, lambda *_: (0,) * len(shape), pipeline_mode=pl.Buffered(1))


def _layer_slice(layer, shape):
    return pl.BlockSpec((None,) + shape, lambda *_: (layer,) + (0,) * len(shape),
                        pipeline_mode=pl.Buffered(1))


BF16_SUBLANES = 16


class _Cast:
    def __init__(self, stacked, layer, n_steps):
        _, rows, cols = stacked.shape
        visits = 1
        while (rows * visits) % (n_steps * BF16_SUBLANES):
            visits *= 2
        block_rows = rows * visits // n_steps
        self.operand = stacked
        self.in_spec = pl.BlockSpec((None, block_rows, cols), lambda i: (layer, i // visits, 0))
        self.out_spec = pl.BlockSpec((block_rows, cols), lambda i: (i // visits, 0))
        self.out_shape = jax.ShapeDtypeStruct((rows, cols), jnp.bfloat16)


def _run_casts(srcs, dsts):
    for src, dst in zip(srcs, dsts, strict=True):
        dst[...] = src[...].astype(dst.dtype)


PROJ_COLS = 512


def _proj_kernel(x_ref, g_ref, w_ref, lng_ref, lnb_ref, sw_ref, sb_ref, bg_ref, *refs, n_casts):
    qkv_ref, sgu_ref, gate_ref = refs[n_casts:n_casts + 3]
    _run_casts(refs[:n_casts], refs[n_casts + 3:])
    xn = _rmsnorm_f32(x_ref[...], g_ref[...]).astype(jnp.bfloat16)

    def project(start):
        return jnp.dot(xn, w_ref[:, start:start + PROJ_COLS], preferred_element_type=jnp.float32)

    gu = _gelu_tanh(project(QKV_WIDTH))
    gv = _gelu_tanh(project(QKV_WIDTH + SGU_WIDTH))
    mu = jnp.mean(gv, axis=-1, keepdims=True)
    dv = gv - mu
    var = jnp.mean(dv * dv, axis=-1, keepdims=True)
    vn = (dv * lax.rsqrt(var + EPS) * lng_ref[...] + lnb_ref[...]).astype(jnp.bfloat16)

    for c in range(0, GATE_WIDTH, PROJ_COLS):
        logits = project(QKV_WIDTH + UV_WIDTH + c) + bg_ref[:, c:c + PROJ_COLS]
        gate_ref[:, c:c + PROJ_COLS] = jax.nn.sigmoid(logits).astype(gate_ref.dtype)
    for c in range(0, QKV_WIDTH, PROJ_COLS):
        qkv_ref[:, c:c + PROJ_COLS] = project(c).astype(qkv_ref.dtype)

    t_to = lax.broadcasted_iota(jnp.int32, (SGU_CHUNK, SGU_CHUNK), 0)
    t_from = lax.broadcasted_iota(jnp.int32, (SGU_CHUNK, SGU_CHUNK), 1)
    for g in range(SGU_GROUPS):
        cols = slice(g * SGU_GROUP_DIM, (g + 1) * SGU_GROUP_DIM)
        wg = jnp.where(t_to >= t_from, sw_ref[g], 0.0).astype(jnp.bfloat16)
        for c in range(ROW_BLOCK // SGU_CHUNK):
            rows = slice(c * SGU_CHUNK, (c + 1) * SGU_CHUNK)
            mixed = jnp.dot(wg, vn[rows, cols], preferred_element_type=jnp.float32) + sb_ref[:, cols]
            sgu_ref[rows, cols] = (gu[rows, cols] * mixed).astype(sgu_ref.dtype)


def _proj(layer, h, g, w, ln_g, ln_b, sgu_w, sgu_b_full, b_gate, casts):
    t = h.shape[0]
    row = lambda i: (i, 0)
    return pl.pallas_call(
        functools.partial(_proj_kernel, n_casts=len(casts)),
        grid=(t // ROW_BLOCK,),
        in_specs=[pl.BlockSpec((ROW_BLOCK, D_MODEL), row),
                  _resident((1, D_MODEL)),
                  _resident((D_MODEL, IN_WIDTH)),
                  _resident((1, SGU_WIDTH)), _resident((1, SGU_WIDTH)),
                  _layer_slice(layer, (SGU_GROUPS, SGU_CHUNK, SGU_CHUNK)),
                  _resident((SGU_CHUNK, SGU_WIDTH)),
                  _resident((1, GATE_WIDTH))] + [c.in_spec for c in casts],
        out_specs=[pl.BlockSpec((ROW_BLOCK, QKV_WIDTH), row),
                   pl.BlockSpec((ROW_BLOCK, SGU_WIDTH), row),
                   pl.BlockSpec((ROW_BLOCK, GATE_WIDTH), row)] + [c.out_spec for c in casts],
        out_shape=[jax.ShapeDtypeStruct((t, QKV_WIDTH), jnp.bfloat16),
                   jax.ShapeDtypeStruct((t, SGU_WIDTH), jnp.bfloat16),
                   jax.ShapeDtypeStruct((t, GATE_WIDTH), jnp.bfloat16)] + [c.out_shape for c in casts],
        compiler_params=_params(1),
        name="proj",
    )(h, g, w, ln_g, ln_b, sgu_w, sgu_b_full, b_gate, *[c.operand for c in casts])


SOFTMAX_ROWS = 16
N_SUB = ROW_BLOCK // ATT_QBLOCK


def _attn_kernel(q_ref, kp_ref, kc_ref, vp_ref, vc_ref, ext_ref, o_ref,
                 kwin, vwin, bias_scr, s_scr, e_scr, l_scr):
    @pl.when((pl.program_id(0) == 0) & (pl.program_id(1) == 0))
    def _expand_bias():
        r = lax.broadcasted_iota(jnp.int32, (ATT_QBLOCK, ATT_WINDOW), 0)
        c = lax.broadcasted_iota(jnp.int32, (ATT_QBLOCK, ATT_WINDOW), 1)
        qc, kc = r // CHUNK, c // CHUNK
        visible = (kc >= qc) & (kc <= qc + LEFT_CHUNKS)
        for h in range(ATT_HEADS):
            rows = jnp.broadcast_to(ext_ref[h], (ATT_QBLOCK, EXT_WIDTH))
            toeplitz = pltpu.roll(rows, EXT_WIDTH - (ATT_QBLOCK - 1), 1, stride=1, stride_axis=0)
            base = jnp.where(visible, toeplitz[:, 0:ATT_WINDOW], MASK_VALUE)
            bias_scr[N_SUB, h] = base
            for j in range(N_SUB):
                bias_scr[j, h] = jnp.where(c < PAST - j * ATT_QBLOCK, MASK_VALUE, base)

    kwin[0:ROW_BLOCK, :] = kp_ref[...]
    kwin[ROW_BLOCK:2 * ROW_BLOCK, :] = kc_ref[...]
    vwin[0:ROW_BLOCK, :] = vp_ref[...]
    vwin[ROW_BLOCK:2 * ROW_BLOCK, :] = vc_ref[...]
    first_block = pl.program_id(1) == 0
    lane = lax.broadcasted_iota(jnp.int32, (ATT_QBLOCK, LANES), 1)
    low_half = lane < ATT_HEAD_DIM
    scale = jnp.bfloat16(ATT_HEAD_DIM ** -0.5)

    def scores(j):
        slot = j % 2
        for hp in range(ATT_HEADS // 2):
            cols = slice(hp * LANES, (hp + 1) * LANES)
            qp = q_ref[j * ATT_QBLOCK:(j + 1) * ATT_QBLOCK, cols] * scale
            kw = kwin[j * ATT_QBLOCK:j * ATT_QBLOCK + ATT_WINDOW, cols]
            for half in range(2):
                h = 2 * hp + half
                keep = low_half if half == 0 else jnp.logical_not(low_half)
                qh = jnp.where(keep, qp, jnp.zeros_like(qp))
                s_scr[slot, h] = lax.dot_general(qh, kw, (((1,), (1,)), ((), ())),
                                                 preferred_element_type=jnp.float32)

    def softmax(j):
        slot = j % 2
        variant = jnp.where(first_block, j, N_SUB)
        for h in range(ATT_HEADS):
            for g in range(ATT_QBLOCK // SOFTMAX_ROWS):
                rows = slice(g * SOFTMAX_ROWS, (g + 1) * SOFTMAX_ROWS)
                s = s_scr[slot, h, rows, :] + bias_scr[variant, h, rows, :]
                e = jnp.exp(s - jnp.max(s, axis=-1, keepdims=True))
                e_scr[slot, h, rows, :] = e.astype(jnp.bfloat16)
                l_scr[slot, h, rows, :] = jnp.broadcast_to(
                    jnp.sum(e, axis=-1, keepdims=True), (SOFTMAX_ROWS, LANES))

    def values(j):
        slot = j % 2
        for hp in range(ATT_HEADS // 2):
            cols = slice(hp * LANES, (hp + 1) * LANES)
            vw = vwin[j * ATT_QBLOCK:j * ATT_QBLOCK + ATT_WINDOW, cols]
            o_lo = jnp.dot(e_scr[slot, 2 * hp], vw, preferred_element_type=jnp.float32)
            o_hi = jnp.dot(e_scr[slot, 2 * hp + 1], vw, preferred_element_type=jnp.float32)
            denom = jnp.where(low_half, l_scr[slot, 2 * hp], l_scr[slot, 2 * hp + 1])
            o = jnp.where(low_half, o_lo, o_hi) * (1.0 / denom)
            o_ref[j * ATT_QBLOCK:(j + 1) * ATT_QBLOCK, cols] = o.astype(o_ref.dtype)

    scores(0)
    for j in range(N_SUB):
        if j + 1 < N_SUB:
            scores(j + 1)
        softmax(j)
        values(j)


def _attention(qkv, ext, batch, seq):
    t = qkv.shape[0]
    nb = seq // ROW_BLOCK
    cur = lambda col: (lambda b, i: (b * nb + i, col))
    prev = lambda col: (lambda b, i: (b * nb + jnp.maximum(i - 1, 0), col))
    blk = (ROW_BLOCK, ATT_WIDTH)
    return pl.pallas_call(
        _attn_kernel,
        grid=(batch, nb),
        in_specs=[pl.BlockSpec(blk, cur(0)),
                  pl.BlockSpec(blk, prev(1)), pl.BlockSpec(blk, cur(1)),
                  pl.BlockSpec(blk, prev(2)), pl.BlockSpec(blk, cur(2)),
                  _resident((ATT_HEADS, 1, EXT_WIDTH))],
        out_specs=pl.BlockSpec(blk, cur(0)),
        out_shape=jax.ShapeDtypeStruct((t, ATT_WIDTH), jnp.bfloat16),
        scratch_shapes=[pltpu.VMEM((2 * ROW_BLOCK, ATT_WIDTH), jnp.bfloat16),
                        pltpu.VMEM((2 * ROW_BLOCK, ATT_WIDTH), jnp.bfloat16),
                        pltpu.VMEM((N_SUB + 1, ATT_HEADS, ATT_QBLOCK, ATT_WINDOW), jnp.float32),
                        pltpu.VMEM((2, ATT_HEADS, ATT_QBLOCK, ATT_WINDOW), jnp.float32),
                        pltpu.VMEM((2, ATT_HEADS, ATT_QBLOCK, ATT_WINDOW), jnp.bfloat16),
                        pltpu.VMEM((2, ATT_HEADS, ATT_QBLOCK, LANES), jnp.float32)],
        compiler_params=_params(2),
        name="attention",
    )(qkv, qkv, qkv, qkv, qkv, ext)


def _extended_rel_bias(rel_bias):
    d_first = PAST + ATT_QBLOCK - 1
    d_last = d_first - (EXT_WIDTH - 1)
    assert d_first >= MAX_REL_DIST and d_last == -MAX_REL_DIST
    n_clipped = d_first - MAX_REL_DIST
    table = rel_bias.astype(jnp.float32)
    top = jnp.broadcast_to(table[:, 2 * MAX_REL_DIST:], (ATT_HEADS, n_clipped))
    return jnp.concatenate([top, table[:, ::-1]], axis=1)[:, None, :]


def _post_kernel(att_ref, sgu_ref, gate_ref, h_ref, wa_ref, ws_ref, wo_ref,
                 g_ref, wi_ref, wf_ref, gf_ref, *refs, final_norm, n_casts):
    o_ref = refs[n_casts]
    _run_casts(refs[:n_casts], refs[n_casts + 1:])
    br_att = jnp.dot(att_ref[...], wa_ref[...], preferred_element_type=jnp.float32)
    br_sgu = jnp.dot(sgu_ref[...], ws_ref[...], preferred_element_type=jnp.float32)
    g_att = gate_ref[:, 0:D_MODEL].astype(jnp.float32)
    g_sgu = gate_ref[:, D_MODEL:GATE_WIDTH].astype(jnp.float32)
    merged = (g_att * br_att + g_sgu * br_sgu).astype(jnp.bfloat16)
    h = h_ref[...] + jnp.dot(merged, wo_ref[...], preferred_element_type=jnp.float32)
    hn = _rmsnorm_f32(h, g_ref[...]).astype(jnp.bfloat16)
    gate = jnp.dot(hn, wi_ref[:, 0:D_FF], preferred_element_type=jnp.float32)
    up = jnp.dot(hn, wi_ref[:, D_FF:2 * D_FF], preferred_element_type=jnp.float32)
    act = (gate * jax.nn.sigmoid(gate) * up).astype(jnp.bfloat16)
    out = h + jnp.dot(act, wf_ref[...], preferred_element_type=jnp.float32)
    if final_norm:
        out = _rmsnorm_f32(out, gf_ref[...])
    o_ref[...] = out


def _post(att, sgu, gate, h, wa, ws, wo, g, wi, wf, g_final, final_norm, casts):
    t = h.shape[0]
    row = lambda i: (i, 0)
    return pl.pallas_call(
        functools.partial(_post_kernel, final_norm=final_norm, n_casts=len(casts)),
        grid=(t // ROW_BLOCK,),
        in_specs=[pl.BlockSpec((ROW_BLOCK, ATT_WIDTH), row),
                  pl.BlockSpec((ROW_BLOCK, SGU_WIDTH), row),
                  pl.BlockSpec((ROW_BLOCK, GATE_WIDTH), row),
                  pl.BlockSpec((ROW_BLOCK, D_MODEL), row),
                  _resident((ATT_WIDTH, D_MODEL)),
                  _resident((SGU_WIDTH, D_MODEL)),
                  _resident((D_MODEL, D_MODEL)),
                  _resident((1, D_MODEL)),
                  _resident((D_MODEL, 2 * D_FF)),
                  _resident((D_FF, D_MODEL)),
                  _resident((1, D_MODEL))] + [c.in_spec for c in casts],
        out_specs=[pl.BlockSpec((ROW_BLOCK, D_MODEL), row)] + [c.out_spec for c in casts],
        out_shape=[jax.ShapeDtypeStruct((t, D_MODEL), jnp.float32)] + [c.out_shape for c in casts],
        compiler_params=_params(1),
        name="post_final" if final_norm else "post",
    )(att, sgu, gate, h, wa, ws, wo, g, wi, wf, g_final, *[c.operand for c in casts])


def kernel(x, norm_mix, w_in, att_rel_bias, sgu_norm_gain, sgu_norm_bias, sgu_w, sgu_b, w_br_att, w_br_sgu, b_gate, w_out, norm_ffn, w_ffn_in, w_ffn_out, norm_final):
    batch, seq, d = x.shape
    depth = w_in.shape[0]
    assert d == D_MODEL and seq % ROW_BLOCK == 0
    n_steps = batch * seq // ROW_BLOCK
    h = x.reshape(batch * seq, d)
    w_in_l = w_in[0].astype(jnp.bfloat16)
    for l in range(depth):
        sgu_b_full = jnp.repeat(sgu_b[l].T, SGU_GROUP_DIM, axis=1)
        post_weights = [_Cast(w, l, n_steps) for w in (w_br_att, w_br_sgu, w_out, w_ffn_in, w_ffn_out)]
        qkv, sgu, gate, wa, ws, wo, wi, wf = _proj(
            l, h, norm_mix[l][None], w_in_l, sgu_norm_gain[l][None], sgu_norm_bias[l][None],
            sgu_w, sgu_b_full, b_gate[l].reshape(1, GATE_WIDTH), post_weights)
        att = _attention(qkv, _extended_rel_bias(att_rel_bias[l]), batch, seq)
        last = l == depth - 1
        h, *next_w_in = _post(att, sgu, gate, h, wa, ws, wo, norm_ffn[l][None], wi, wf,
                              norm_final[None], final_norm=last,
                              casts=[] if last else [_Cast(w_in, l + 1, n_steps)])
        if not last:
            w_in_l, = next_w_in
    return h.reshape(batch, seq, d)
```

```python
import functools
import math

import jax
import jax.numpy as jnp
from jax import lax
from jax.experimental import pallas as pl
from jax.experimental.pallas import tpu as pltpu

D_MODEL = 1024
CHUNK = 64
LEFT_CHUNKS = 8
ATT_HEADS = 8
ATT_HEAD_DIM = 64
ATT_WIDTH = ATT_HEADS * ATT_HEAD_DIM
MAX_REL_DIST = 128
SGU_CHUNK = 128
SGU_GROUPS = 4
SGU_WIDTH = 512
SGU_GROUP_DIM = SGU_WIDTH // SGU_GROUPS
N_BRANCHES = 2
GATE_WIDTH = N_BRANCHES * D_MODEL
QKV_WIDTH = 3 * ATT_WIDTH
UV_WIDTH = 2 * SGU_WIDTH
IN_WIDTH = QKV_WIDTH + UV_WIDTH + GATE_WIDTH
D_FF = 2816
EPS = 1e-6
MASK_VALUE = -1e30

V7X_VMEM_BYTES = 64 * 1024 * 1024
VMEM_LIMIT_BYTES = V7X_VMEM_BYTES - 8 * 1024 * 1024
LANES = 128

ROW_BLOCK = 512
ATT_QBLOCK = 2 * CHUNK
ATT_WINDOW = (LEFT_CHUNKS + 2) * CHUNK
PAST = LEFT_CHUNKS * CHUNK
EXT_WIDTH = ATT_WINDOW + ATT_QBLOCK


def _params(n_axes, flags=None):
    return pltpu.CompilerParams(
        dimension_semantics=("arbitrary",) * n_axes,
        vmem_limit_bytes=VMEM_LIMIT_BYTES,
        flags=flags)


def _rmsnorm_f32(x, g):
    return x * lax.rsqrt(jnp.mean(x * x, axis=-1, keepdims=True) + EPS) * g


def _gelu_tanh(x):
    c = math.sqrt(2.0 / math.pi)
    return 0.5 * x * (1.0 + jnp.tanh(c * (x + 0.044715 * (x * x * x))))


def _resident(shape):
    return pl.BlockSpec(shape, lambda *_: (0,) * len(shape), pipeline_mode=pl.Buffered(1))


def _layer_slice(layer, shape):
    return pl.BlockSpec((None,) + shape, lambda *_: (layer,) + (0,) * len(shape),
                        pipeline_mode=pl.Buffered(1))


BF16_SUBLANES = 16


class _Cast:
    def __init__(self, stacked, layer, n_steps):
        _, rows, cols = stacked.shape
        visits = 1
        while (rows * visits) % (n_steps * BF16_SUBLANES):
            visits *= 2
        block_rows = rows * visits // n_steps
        self.operand = stacked
        self.in_spec = pl.BlockSpec((None, block_rows, cols), lambda i: (layer, i // visits, 0))
        self.out_spec = pl.BlockSpec((block_rows, cols), lambda i: (i // visits, 0))
        self.out_shape = jax.ShapeDtypeStruct((rows, cols), jnp.bfloat16)


def _run_casts(srcs, dsts):
    for src, dst in zip(srcs, dsts, strict=True):
        dst[...] = src[...].astype(dst.dtype)


PROJ_COLS = 512
PROJ_ROWS = 1024


def _proj_kernel(x_ref, g_ref, w_ref, lng_ref, lnb_ref, sw_ref, sb_ref, bg_ref, *refs, n_casts):
    qkv_ref, sgu_ref, gate_ref = refs[n_casts:n_casts + 3]
    _run_casts(refs[:n_casts], refs[n_casts + 3:])
    xn = _rmsnorm_f32(x_ref[...], g_ref[...]).astype(jnp.bfloat16)

    def project(start):
        return jnp.dot(xn, w_ref[:, start:start + PROJ_COLS], preferred_element_type=jnp.float32)

    gu = _gelu_tanh(project(QKV_WIDTH))
    gv = _gelu_tanh(project(QKV_WIDTH + SGU_WIDTH))
    mu = jnp.mean(gv, axis=-1, keepdims=True)
    dv = gv - mu
    var = jnp.mean(dv * dv, axis=-1, keepdims=True)
    vn = (dv * lax.rsqrt(var + EPS) * lng_ref[...] + lnb_ref[...]).astype(jnp.bfloat16)

    for c in range(0, GATE_WIDTH, PROJ_COLS):
        logits = project(QKV_WIDTH + UV_WIDTH + c) + bg_ref[:, c:c + PROJ_COLS]
        gate_ref[:, c:c + PROJ_COLS] = jax.nn.sigmoid(logits).astype(gate_ref.dtype)
    for c in range(0, QKV_WIDTH, PROJ_COLS):
        qkv_ref[:, c:c + PROJ_COLS] = project(c).astype(qkv_ref.dtype)

    t_to = lax.broadcasted_iota(jnp.int32, (SGU_CHUNK, SGU_CHUNK), 0)
    t_from = lax.broadcasted_iota(jnp.int32, (SGU_CHUNK, SGU_CHUNK), 1)
    for g in range(SGU_GROUPS):
        cols = slice(g * SGU_GROUP_DIM, (g + 1) * SGU_GROUP_DIM)
        wg = jnp.where(t_to >= t_from, sw_ref[g], 0.0).astype(jnp.bfloat16)
        for c in range(PROJ_ROWS // SGU_CHUNK):
            rows = slice(c * SGU_CHUNK, (c + 1) * SGU_CHUNK)
            mixed = jnp.dot(wg, vn[rows, cols], preferred_element_type=jnp.float32) + sb_ref[:, cols]
            sgu_ref[rows, cols] = (gu[rows, cols] * mixed).astype(sgu_ref.dtype)


def _proj(layer, h, g, w, ln_g, ln_b, sgu_w, sgu_b_full, b_gate, casts):
    t = h.shape[0]
    row = lambda i: (i, 0)
    return pl.pallas_call(
        functools.partial(_proj_kernel, n_casts=len(casts)),
        grid=(t // PROJ_ROWS,),
        in_specs=[pl.BlockSpec((PROJ_ROWS, D_MODEL), row),
                  _resident((1, D_MODEL)),
                  _resident((D_MODEL, IN_WIDTH)),
                  _resident((1, SGU_WIDTH)), _resident((1, SGU_WIDTH)),
                  _layer_slice(layer, (SGU_GROUPS, SGU_CHUNK, SGU_CHUNK)),
                  _resident((SGU_CHUNK, SGU_WIDTH)),
                  _resident((1, GATE_WIDTH))] + [c.in_spec for c in casts],
        out_specs=[pl.BlockSpec((PROJ_ROWS, QKV_WIDTH), row),
                   pl.BlockSpec((PROJ_ROWS, SGU_WIDTH), row),
                   pl.BlockSpec((PROJ_ROWS, GATE_WIDTH), row)] + [c.out_spec for c in casts],
        out_shape=[jax.ShapeDtypeStruct((t, QKV_WIDTH), jnp.bfloat16),
                   jax.ShapeDtypeStruct((t, SGU_WIDTH), jnp.bfloat16),
                   jax.ShapeDtypeStruct((t, GATE_WIDTH), jnp.bfloat16)] + [c.out_shape for c in casts],
        compiler_params=_params(1),
        name="proj",
    )(h, g, w, ln_g, ln_b, sgu_w, sgu_b_full, b_gate, *[c.operand for c in casts])


SOFTMAX_ROWS = 16
N_SUB = ROW_BLOCK // ATT_QBLOCK


def _attn_kernel(q_ref, kp_ref, kc_ref, vp_ref, vc_ref, ext_ref, o_ref,
                 kwin, vwin, bias_scr, s_scr, e_scr, l_scr):
    @pl.when((pl.program_id(0) == 0) & (pl.program_id(1) == 0))
    def _expand_bias():
        r = lax.broadcasted_iota(jnp.int32, (ATT_QBLOCK, ATT_WINDOW), 0)
        c = lax.broadcasted_iota(jnp.int32, (ATT_QBLOCK, ATT_WINDOW), 1)
        qc, kc = r // CHUNK, c // CHUNK
        visible = (kc >= qc) & (kc <= qc + LEFT_CHUNKS)
        for h in range(ATT_HEADS):
            rows = jnp.broadcast_to(ext_ref[h], (ATT_QBLOCK, EXT_WIDTH))
            toeplitz = pltpu.roll(rows, EXT_WIDTH - (ATT_QBLOCK - 1), 1, stride=1, stride_axis=0)
            base = jnp.where(visible, toeplitz[:, 0:ATT_WINDOW], MASK_VALUE)
            bias_scr[N_SUB, h] = base
            for j in range(N_SUB):
                bias_scr[j, h] = jnp.where(c < PAST - j * ATT_QBLOCK, MASK_VALUE, base)

    kwin[0:ROW_BLOCK, :] = kp_ref[...]
    kwin[ROW_BLOCK:2 * ROW_BLOCK, :] = kc_ref[...]
    vwin[0:ROW_BLOCK, :] = vp_ref[...]
    vwin[ROW_BLOCK:2 * ROW_BLOCK, :] = vc_ref[...]
    first_block = pl.program_id(1) == 0
    lane = lax.broadcasted_iota(jnp.int32, (ATT_QBLOCK, LANES), 1)
    low_half = lane < ATT_HEAD_DIM
    scale = jnp.bfloat16(ATT_HEAD_DIM ** -0.5)

    def scores(j):
        slot = j % 2
        for hp in range(ATT_HEADS // 2):
            cols = slice(hp * LANES, (hp + 1) * LANES)
            qp = q_ref[j * ATT_QBLOCK:(j + 1) * ATT_QBLOCK, cols] * scale
            kw = kwin[j * ATT_QBLOCK:j * ATT_QBLOCK + ATT_WINDOW, cols]
            for half in range(2):
                h = 2 * hp + half
                keep = low_half if half == 0 else jnp.logical_not(low_half)
                qh = jnp.where(keep, qp, jnp.zeros_like(qp))
                s_scr[slot, h] = lax.dot_general(qh, kw, (((1,), (1,)), ((), ())),
                                                 preferred_element_type=jnp.float32)

    def softmax(j):
        slot = j % 2
        variant = jnp.where(first_block, j, N_SUB)
        for h in range(ATT_HEADS):
            for g in range(ATT_QBLOCK // SOFTMAX_ROWS):
                rows = slice(g * SOFTMAX_ROWS, (g + 1) * SOFTMAX_ROWS)
                s = s_scr[slot, h, rows, :] + bias_scr[variant, h, rows, :]
                e = jnp.exp(s - jnp.max(s, axis=-1, keepdims=True))
                e_scr[slot, h, rows, :] = e.astype(jnp.bfloat16)
                l_scr[slot, h, rows, :] = jnp.broadcast_to(
                    jnp.sum(e, axis=-1, keepdims=True), (SOFTMAX_ROWS, LANES))

    def values(j):
        slot = j % 2
        for hp in range(ATT_HEADS // 2):
            cols = slice(hp * LANES, (hp + 1) * LANES)
            vw = vwin[j * ATT_QBLOCK:j * ATT_QBLOCK + ATT_WINDOW, cols]
            o_lo = jnp.dot(e_scr[slot, 2 * hp], vw, preferred_element_type=jnp.float32)
            o_hi = jnp.dot(e_scr[slot, 2 * hp + 1], vw, preferred_element_type=jnp.float32)
            denom = jnp.where(low_half, l_scr[slot, 2 * hp], l_scr[slot, 2 * hp + 1])
            o = jnp.where(low_half, o_lo, o_hi) * (1.0 / denom)
            o_ref[j * ATT_QBLOCK:(j + 1) * ATT_QBLOCK, cols] = o.astype(o_ref.dtype)

    scores(0)
    for j in range(N_SUB):
        if j + 1 < N_SUB:
            scores(j + 1)
        softmax(j)
        values(j)


def _attention(qkv, ext, batch, seq):
    t = qkv.shape[0]
    nb = seq // ROW_BLOCK
    cur = lambda col: (lambda b, i: (b * nb + i, col))
    prev = lambda col: (lambda b, i: (b * nb + jnp.maximum(i - 1, 0), col))
    blk = (ROW_BLOCK, ATT_WIDTH)
    return pl.pallas_call(
        _attn_kernel,
        grid=(batch, nb),
        in_specs=[pl.BlockSpec(blk, cur(0)),
                  pl.BlockSpec(blk, prev(1)), pl.BlockSpec(blk, cur(1)),
                  pl.BlockSpec(blk, prev(2)), pl.BlockSpec(blk, cur(2)),
                  _resident((ATT_HEADS, 1, EXT_WIDTH))],
        out_specs=pl.BlockSpec(blk, cur(0)),
        out_shape=jax.ShapeDtypeStruct((t, ATT_WIDTH), jnp.bfloat16),
        scratch_shapes=[pltpu.VMEM((2 * ROW_BLOCK, ATT_WIDTH), jnp.bfloat16),
                        pltpu.VMEM((2 * ROW_BLOCK, ATT_WIDTH), jnp.bfloat16),
                        pltpu.VMEM((N_SUB + 1, ATT_HEADS, ATT_QBLOCK, ATT_WINDOW), jnp.float32),
                        pltpu.VMEM((2, ATT_HEADS, ATT_QBLOCK, ATT_WINDOW), jnp.float32),
                        pltpu.VMEM((2, ATT_HEADS, ATT_QBLOCK, ATT_WINDOW), jnp.bfloat16),
                        pltpu.VMEM((2, ATT_HEADS, ATT_QBLOCK, LANES), jnp.float32)],
        compiler_params=_params(2),
        name="attention",
    )(qkv, qkv, qkv, qkv, qkv, ext)


def _extended_rel_bias(rel_bias):
    d_first = PAST + ATT_QBLOCK - 1
    d_last = d_first - (EXT_WIDTH - 1)
    assert d_first >= MAX_REL_DIST and d_last == -MAX_REL_DIST
    n_clipped = d_first - MAX_REL_DIST
    table = rel_bias.astype(jnp.float32)
    top = jnp.broadcast_to(table[:, 2 * MAX_REL_DIST:], (ATT_HEADS, n_clipped))
    return jnp.concatenate([top, table[:, ::-1]], axis=1)[:, None, :]


def _post_kernel(att_ref, sgu_ref, gate_ref, h_ref, wa_ref, ws_ref, wo_ref,
                 g_ref, wi_ref, wf_ref, gf_ref, *refs, final_norm, n_casts):
    o_ref = refs[n_casts]
    _run_casts(refs[:n_casts], refs[n_casts + 1:])
    br_att = jnp.dot(att_ref[...], wa_ref[...], preferred_element_type=jnp.float32)
    br_sgu = jnp.dot(sgu_ref[...], ws_ref[...], preferred_element_type=jnp.float32)
    g_att = gate_ref[:, 0:D_MODEL].astype(jnp.float32)
    g_sgu = gate_ref[:, D_MODEL:GATE_WIDTH].astype(jnp.float32)
    merged = (g_att * br_att + g_sgu * br_sgu).astype(jnp.bfloat16)
    h = h_ref[...] + jnp.dot(merged, wo_ref[...], preferred_element_type=jnp.float32)
    hn = _rmsnorm_f32(h, g_ref[...]).astype(jnp.bfloat16)
    gate = jnp.dot(hn, wi_ref[:, 0:D_FF], preferred_element_type=jnp.float32)
    up = jnp.dot(hn, wi_ref[:, D_FF:2 * D_FF], preferred_element_type=jnp.float32)
    act = (gate * jax.nn.sigmoid(gate) * up).astype(jnp.bfloat16)
    out = h + jnp.dot(act, wf_ref[...], preferred_element_type=jnp.float32)
    if final_norm:
        out = _rmsnorm_f32(out, gf_ref[...])
    o_ref[...] = out


def _post(att, sgu, gate, h, wa, ws, wo, g, wi, wf, g_final, final_norm, casts):
    t = h.shape[0]
    row = lambda i: (i, 0)
    return pl.pallas_call(
        functools.partial(_post_kernel, final_norm=final_norm, n_casts=len(casts)),
        grid=(t // ROW_BLOCK,),
        in_specs=[pl.BlockSpec((ROW_BLOCK, ATT_WIDTH), row),
                  pl.BlockSpec((ROW_BLOCK, SGU_WIDTH), row),
                  pl.BlockSpec((ROW_BLOCK, GATE_WIDTH), row),
                  pl.BlockSpec((ROW_BLOCK, D_MODEL), row),
                  _resident((ATT_WIDTH, D_MODEL)),
                  _resident((SGU_WIDTH, D_MODEL)),
                  _resident((D_MODEL, D_MODEL)),
                  _resident((1, D_MODEL)),
                  _resident((D_MODEL, 2 * D_FF)),
                  _resident((D_FF, D_MODEL)),
                  _resident((1, D_MODEL))] + [c.in_spec for c in casts],
        out_specs=[pl.BlockSpec((ROW_BLOCK, D_MODEL), row)] + [c.out_spec for c in casts],
        out_shape=[jax.ShapeDtypeStruct((t, D_MODEL), jnp.float32)] + [c.out_shape for c in casts],
        compiler_params=_params(1),
        name="post_final" if final_norm else "post",
    )(att, sgu, gate, h, wa, ws, wo, g, wi, wf, g_final, *[c.operand for c in casts])


def kernel(x, norm_mix, w_in, att_rel_bias, sgu_norm_gain, sgu_norm_bias, sgu_w, sgu_b, w_br_att, w_br_sgu, b_gate, w_out, norm_ffn, w_ffn_in, w_ffn_out, norm_final):
    batch, seq, d = x.shape
    depth = w_in.shape[0]
    assert d == D_MODEL and seq % ROW_BLOCK == 0
    n_steps = batch * seq // ROW_BLOCK
    h = x.reshape(batch * seq, d)
    w_in_l = w_in[0].astype(jnp.bfloat16)
    for l in range(depth):
        sgu_b_full = jnp.repeat(sgu_b[l].T, SGU_GROUP_DIM, axis=1)
        post_weights = [_Cast(w, l, batch * seq // PROJ_ROWS)
                        for w in (w_br_att, w_br_sgu, w_out, w_ffn_in, w_ffn_out)]
        qkv, sgu, gate, wa, ws, wo, wi, wf = _proj(
            l, h, norm_mix[l][None], w_in_l, sgu_norm_gain[l][None], sgu_norm_bias[l][None],
            sgu_w, sgu_b_full, b_gate[l].reshape(1, GATE_WIDTH), post_weights)
        att = _attention(qkv, _extended_rel_bias(att_rel_bias[l]), batch, seq)
        last = l == depth - 1
        h, *next_w_in = _post(att, sgu, gate, h, wa, ws, wo, norm_ffn[l][None], wi, wf,
                              norm_final[None], final_norm=last,
                              casts=[] if last else [_Cast(w_in, l + 1, n_steps)])
        if not last:
            w_in_l, = next_w_in
    return h.reshape(batch, seq, d)
```

```python
import functools
import math

import jax
import jax.numpy as jnp
from jax import lax
from jax.experimental import pallas as pl
from jax.experimental.pallas import tpu as pltpu

D_MODEL = 1024
CHUNK = 64
LEFT_CHUNKS = 8
ATT_HEADS = 8
ATT_HEAD_DIM = 64
ATT_WIDTH = ATT_HEADS * ATT_HEAD_DIM
MAX_REL_DIST = 128
SGU_CHUNK = 128
SGU_GROUPS = 4
SGU_WIDTH = 512
SGU_GROUP_DIM = SGU_WIDTH // SGU_GROUPS
N_BRANCHES = 2
GATE_WIDTH = N_BRANCHES * D_MODEL
QKV_WIDTH = 3 * ATT_WIDTH
UV_WIDTH = 2 * SGU_WIDTH
IN_WIDTH = QKV_WIDTH + UV_WIDTH + GATE_WIDTH
D_FF = 2816
EPS = 1e-6
MASK_VALUE = -1e30
LOG2E = math.log2(math.e)
Q_SCALE = ATT_HEAD_DIM ** -0.5 * LOG2E

V7X_VMEM_BYTES = 64 * 1024 * 1024
VMEM_LIMIT_BYTES = V7X_VMEM_BYTES - 8 * 1024 * 1024
LANES = 128

ROW_BLOCK = 512
ATT_QBLOCK = 2 * CHUNK
ATT_WINDOW = (LEFT_CHUNKS + 2) * CHUNK
PAST = LEFT_CHUNKS * CHUNK
EXT_WIDTH = ATT_WINDOW + ATT_QBLOCK


def _params(n_axes, flags=None):
    return pltpu.CompilerParams(
        dimension_semantics=("arbitrary",) * n_axes,
        vmem_limit_bytes=VMEM_LIMIT_BYTES,
        flags=flags)


def _rmsnorm_f32(x, g):
    return x * lax.rsqrt(jnp.mean(x * x, axis=-1, keepdims=True) + EPS) * g


def _gelu_tanh(x):
    c = math.sqrt(2.0 / math.pi)
    return 0.5 * x * (1.0 + jnp.tanh(c * (x + 0.044715 * (x * x * x))))


def _resident(shape):
    return pl.BlockSpec(shape, lambda *_: (0,) * len(shape), pipeline_mode=pl.Buffered(1))


def _layer_slice(layer, shape):
    return pl.BlockSpec((None,) + shape, lambda *_: (layer,) + (0,) * len(shape),
                        pipeline_mode=pl.Buffered(1))


BF16_SUBLANES = 16


class _Cast:
    def __init__(self, stacked, layer, n_steps):
        _, rows, cols = stacked.shape
        visits = 1
        while (rows * visits) % (n_steps * BF16_SUBLANES):
            visits *= 2
        block_rows = rows * visits // n_steps
        self.operand = stacked
        self.in_spec = pl.BlockSpec((None, block_rows, cols), lambda i: (layer, i // visits, 0))
        self.out_spec = pl.BlockSpec((block_rows, cols), lambda i: (i // visits, 0))
        self.out_shape = jax.ShapeDtypeStruct((rows, cols), jnp.bfloat16)


def _run_casts(srcs, dsts):
    for src, dst in zip(srcs, dsts, strict=True):
        dst[...] = src[...].astype(dst.dtype)


PROJ_COLS = 512
PROJ_ROWS = 1024


def _proj_kernel(x_ref, g_ref, w_ref, lng_ref, lnb_ref, sw_ref, sb_ref, bg_ref, *refs, n_casts):
    qkv_ref, sgu_ref, gate_ref = refs[n_casts:n_casts + 3]
    _run_casts(refs[:n_casts], refs[n_casts + 3:])
    xn = _rmsnorm_f32(x_ref[...], g_ref[...]).astype(jnp.bfloat16)

    def project(start):
        return jnp.dot(xn, w_ref[:, start:start + PROJ_COLS], preferred_element_type=jnp.float32)

    gu = _gelu_tanh(project(QKV_WIDTH))
    gv = _gelu_tanh(project(QKV_WIDTH + SGU_WIDTH))
    mu = jnp.mean(gv, axis=-1, keepdims=True)
    dv = gv - mu
    var = jnp.mean(dv * dv, axis=-1, keepdims=True)
    vn = (dv * lax.rsqrt(var + EPS) * lng_ref[...] + lnb_ref[...]).astype(jnp.bfloat16)

    for c in range(0, GATE_WIDTH, PROJ_COLS):
        logits = project(QKV_WIDTH + UV_WIDTH + c) + bg_ref[:, c:c + PROJ_COLS]
        gate_ref[:, c:c + PROJ_COLS] = jax.nn.sigmoid(logits).astype(gate_ref.dtype)
    assert PROJ_COLS == ATT_WIDTH
    qkv_ref[:, 0:ATT_WIDTH] = (project(0) * Q_SCALE).astype(qkv_ref.dtype)
    for c in range(ATT_WIDTH, QKV_WIDTH, PROJ_COLS):
        qkv_ref[:, c:c + PROJ_COLS] = project(c).astype(qkv_ref.dtype)

    t_to = lax.broadcasted_iota(jnp.int32, (SGU_CHUNK, SGU_CHUNK), 0)
    t_from = lax.broadcasted_iota(jnp.int32, (SGU_CHUNK, SGU_CHUNK), 1)
    for g in range(SGU_GROUPS):
        cols = slice(g * SGU_GROUP_DIM, (g + 1) * SGU_GROUP_DIM)
        wg = jnp.where(t_to >= t_from, sw_ref[g], 0.0).astype(jnp.bfloat16)
        for c in range(PROJ_ROWS // SGU_CHUNK):
            rows = slice(c * SGU_CHUNK, (c + 1) * SGU_CHUNK)
            mixed = jnp.dot(wg, vn[rows, cols], preferred_element_type=jnp.float32) + sb_ref[:, cols]
            sgu_ref[rows, cols] = (gu[rows, cols] * mixed).astype(sgu_ref.dtype)


def _proj(layer, h, g, w, ln_g, ln_b, sgu_w, sgu_b_full, b_gate, casts):
    t = h.shape[0]
    row = lambda i: (i, 0)
    return pl.pallas_call(
        functools.partial(_proj_kernel, n_casts=len(casts)),
        grid=(t // PROJ_ROWS,),
        in_specs=[pl.BlockSpec((PROJ_ROWS, D_MODEL), row),
                  _resident((1, D_MODEL)),
                  _resident((D_MODEL, IN_WIDTH)),
                  _resident((1, SGU_WIDTH)), _resident((1, SGU_WIDTH)),
                  _layer_slice(layer, (SGU_GROUPS, SGU_CHUNK, SGU_CHUNK)),
                  _resident((SGU_CHUNK, SGU_WIDTH)),
                  _resident((1, GATE_WIDTH))] + [c.in_spec for c in casts],
        out_specs=[pl.BlockSpec((PROJ_ROWS, QKV_WIDTH), row),
                   pl.BlockSpec((PROJ_ROWS, SGU_WIDTH), row),
                   pl.BlockSpec((PROJ_ROWS, GATE_WIDTH), row)] + [c.out_spec for c in casts],
        out_shape=[jax.ShapeDtypeStruct((t, QKV_WIDTH), jnp.bfloat16),
                   jax.ShapeDtypeStruct((t, SGU_WIDTH), jnp.bfloat16),
                   jax.ShapeDtypeStruct((t, GATE_WIDTH), jnp.bfloat16)] + [c.out_shape for c in casts],
        compiler_params=_params(1),
        name="proj",
    )(h, g, w, ln_g, ln_b, sgu_w, sgu_b_full, b_gate, *[c.operand for c in casts])


SOFTMAX_ROWS = 16
N_SUB = ROW_BLOCK // ATT_QBLOCK


def _attn_kernel(q_ref, kp_ref, kc_ref, vp_ref, vc_ref, ext_ref, o_ref,
                 kwin, vwin_lo, vwin_hi, bias_scr, s_scr, e_scr):
    @pl.when((pl.program_id(0) == 0) & (pl.program_id(1) == 0))
    def _expand_bias():
        r = lax.broadcasted_iota(jnp.int32, (ATT_QBLOCK, ATT_WINDOW), 0)
        c = lax.broadcasted_iota(jnp.int32, (ATT_QBLOCK, ATT_WINDOW), 1)
        qc, kc = r // CHUNK, c // CHUNK
        visible = (kc >= qc) & (kc <= qc + LEFT_CHUNKS)
        for h in range(ATT_HEADS):
            rows = jnp.broadcast_to(ext_ref[h], (ATT_QBLOCK, EXT_WIDTH))
            toeplitz = pltpu.roll(rows, EXT_WIDTH - (ATT_QBLOCK - 1), 1, stride=1, stride_axis=0)
            base = jnp.where(visible, toeplitz[:, 0:ATT_WINDOW] * LOG2E, MASK_VALUE)
            bias_scr[N_SUB, h] = base
            for j in range(N_SUB):
                bias_scr[j, h] = jnp.where(c < PAST - j * ATT_QBLOCK, MASK_VALUE, base)

    kwin[0:ROW_BLOCK, :] = kp_ref[...]
    kwin[ROW_BLOCK:2 * ROW_BLOCK, :] = kc_ref[...]
    col = lax.broadcasted_iota(jnp.int32, (ROW_BLOCK, ATT_WIDTH), 1)
    low_cols = (col % LANES) < ATT_HEAD_DIM
    for v_ref, rows in ((vp_ref, slice(0, ROW_BLOCK)), (vc_ref, slice(ROW_BLOCK, 2 * ROW_BLOCK))):
        v = v_ref[...]
        vwin_lo[rows, :] = jnp.where(low_cols, v, jnp.ones_like(v))
        vwin_hi[rows, :] = jnp.where(low_cols, jnp.ones_like(v), v)
    first_block = pl.program_id(1) == 0
    lane = lax.broadcasted_iota(jnp.int32, (ATT_QBLOCK, LANES), 1)
    low_half = lane < ATT_HEAD_DIM

    def scores(j):
        slot = j % 2
        for hp in range(ATT_HEADS // 2):
            cols = slice(hp * LANES, (hp + 1) * LANES)
            qp = q_ref[j * ATT_QBLOCK:(j + 1) * ATT_QBLOCK, cols]
            kw = kwin[j * ATT_QBLOCK:j * ATT_QBLOCK + ATT_WINDOW, cols]
            for half in range(2):
                h = 2 * hp + half
                keep = low_half if half == 0 else jnp.logical_not(low_half)
                qh = jnp.where(keep, qp, jnp.zeros_like(qp))
                s_scr[slot, h] = lax.dot_general(qh, kw, (((1,), (1,)), ((), ())),
                                                 preferred_element_type=jnp.float32)

    def softmax(j):
        slot = j % 2
        variant = jnp.where(first_block, j, N_SUB)
        for h in range(ATT_HEADS):
            for g in range(ATT_QBLOCK // SOFTMAX_ROWS):
                rows = slice(g * SOFTMAX_ROWS, (g + 1) * SOFTMAX_ROWS)
                s = s_scr[slot, h, rows, :] + bias_scr[variant, h, rows, :]
                e = jnp.exp2(s - jnp.max(s, axis=-1, keepdims=True))
                e_scr[slot, h, rows, :] = e.astype(jnp.bfloat16)

    def values(j):
        slot = j % 2
        for hp in range(ATT_HEADS // 2):
            cols = slice(hp * LANES, (hp + 1) * LANES)
            window = slice(j * ATT_QBLOCK, j * ATT_QBLOCK + ATT_WINDOW)
            o_lo = jnp.dot(e_scr[slot, 2 * hp], vwin_lo[window, cols],
                           preferred_element_type=jnp.float32)
            o_hi = jnp.dot(e_scr[slot, 2 * hp + 1], vwin_hi[window, cols],
                           preferred_element_type=jnp.float32)
            denom = pltpu.roll(jnp.where(low_half, o_hi, o_lo), ATT_HEAD_DIM, 1)
            o = jnp.where(low_half, o_lo, o_hi) * (1.0 / denom)
            o_ref[j * ATT_QBLOCK:(j + 1) * ATT_QBLOCK, cols] = o.astype(o_ref.dtype)

    scores(0)
    for j in range(N_SUB):
        if j + 1 < N_SUB:
            scores(j + 1)
        softmax(j)
        values(j)


def _attention(qkv, ext, batch, seq):
    t = qkv.shape[0]
    nb = seq // ROW_BLOCK
    cur = lambda col: (lambda b, i: (b * nb + i, col))
    prev = lambda col: (lambda b, i: (b * nb + jnp.maximum(i - 1, 0), col))
    blk = (ROW_BLOCK, ATT_WIDTH)
    return pl.pallas_call(
        _attn_kernel,
        grid=(batch, nb),
        in_specs=[pl.BlockSpec(blk, cur(0)),
                  pl.BlockSpec(blk, prev(1)), pl.BlockSpec(blk, cur(1)),
                  pl.BlockSpec(blk, prev(2)), pl.BlockSpec(blk, cur(2)),
                  _resident((ATT_HEADS, 1, EXT_WIDTH))],
        out_specs=pl.BlockSpec(blk, cur(0)),
        out_shape=jax.ShapeDtypeStruct((t, ATT_WIDTH), jnp.bfloat16),
        scratch_shapes=[pltpu.VMEM((2 * ROW_BLOCK, ATT_WIDTH), jnp.bfloat16),
                        pltpu.VMEM((2 * ROW_BLOCK, ATT_WIDTH), jnp.bfloat16),
                        pltpu.VMEM((2 * ROW_BLOCK, ATT_WIDTH), jnp.bfloat16),
                        pltpu.VMEM((N_SUB + 1, ATT_HEADS, ATT_QBLOCK, ATT_WINDOW), jnp.float32),
                        pltpu.VMEM((2, ATT_HEADS, ATT_QBLOCK, ATT_WINDOW), jnp.float32),
                        pltpu.VMEM((2, ATT_HEADS, ATT_QBLOCK, ATT_WINDOW), jnp.bfloat16)],
        compiler_params=_params(2),
        name="attention",
    )(qkv, qkv, qkv, qkv, qkv, ext)


def _extended_rel_bias(rel_bias):
    d_first = PAST + ATT_QBLOCK - 1
    d_last = d_first - (EXT_WIDTH - 1)
    assert d_first >= MAX_REL_DIST and d_last == -MAX_REL_DIST
    n_clipped = d_first - MAX_REL_DIST
    table = rel_bias.astype(jnp.float32)
    top = jnp.broadcast_to(table[:, 2 * MAX_REL_DIST:], (ATT_HEADS, n_clipped))
    return jnp.concatenate([top, table[:, ::-1]], axis=1)[:, None, :]


def _post_kernel(att_ref, sgu_ref, gate_ref, h_ref, wa_ref, ws_ref, wo_ref,
                 g_ref, wi_ref, wf_ref, gf_ref, *refs, final_norm, n_casts):
    o_ref = refs[n_casts]
    _run_casts(refs[:n_casts], refs[n_casts + 1:])
    br_att = jnp.dot(att_ref[...], wa_ref[...], preferred_element_type=jnp.float32)
    br_sgu = jnp.dot(sgu_ref[...], ws_ref[...], preferred_element_type=jnp.float32)
    g_att = gate_ref[:, 0:D_MODEL].astype(jnp.float32)
    g_sgu = gate_ref[:, D_MODEL:GATE_WIDTH].astype(jnp.float32)
    merged = (g_att * br_att + g_sgu * br_sgu).astype(jnp.bfloat16)
    h = h_ref[...] + jnp.dot(merged, wo_ref[...], preferred_element_type=jnp.float32)
    hn = _rmsnorm_f32(h, g_ref[...]).astype(jnp.bfloat16)
    gate = jnp.dot(hn, wi_ref[:, 0:D_FF], preferred_element_type=jnp.float32)
    up = jnp.dot(hn, wi_ref[:, D_FF:2 * D_FF], preferred_element_type=jnp.float32)
    act = (gate * jax.nn.sigmoid(gate) * up).astype(jnp.bfloat16)
    out = h + jnp.dot(act, wf_ref[...], preferred_element_type=jnp.float32)
    if final_norm:
        out = _rmsnorm_f32(out, gf_ref[...])
    o_ref[...] = out


def _post(att, sgu, gate, h, wa, ws, wo, g, wi, wf, g_final, final_norm, casts):
    t = h.shape[0]
    row = lambda i: (i, 0)
    return pl.pallas_call(
        functools.partial(_post_kernel, final_norm=final_norm, n_casts=len(casts)),
        grid=(t // ROW_BLOCK,),
        in_specs=[pl.BlockSpec((ROW_BLOCK, ATT_WIDTH), row),
                  pl.BlockSpec((ROW_BLOCK, SGU_WIDTH), row),
                  pl.BlockSpec((ROW_BLOCK, GATE_WIDTH), row),
                  pl.BlockSpec((ROW_BLOCK, D_MODEL), row),
                  _resident((ATT_WIDTH, D_MODEL)),
                  _resident((SGU_WIDTH, D_MODEL)),
                  _resident((D_MODEL, D_MODEL)),
                  _resident((1, D_MODEL)),
                  _resident((D_MODEL, 2 * D_FF)),
                  _resident((D_FF, D_MODEL)),
                  _resident((1, D_MODEL))] + [c.in_spec for c in casts],
        out_specs=[pl.BlockSpec((ROW_BLOCK, D_MODEL), row)] + [c.out_spec for c in casts],
        out_shape=[jax.ShapeDtypeStruct((t, D_MODEL), jnp.float32)] + [c.out_shape for c in casts],
        compiler_params=_params(1),
        name="post_final" if final_norm else "post",
    )(att, sgu, gate, h, wa, ws, wo, g, wi, wf, g_final, *[c.operand for c in casts])


def kernel(x, norm_mix, w_in, att_rel_bias, sgu_norm_gain, sgu_norm_bias, sgu_w, sgu_b, w_br_att, w_br_sgu, b_gate, w_out, norm_ffn, w_ffn_in, w_ffn_out, norm_final):
    batch, seq, d = x.shape
    depth = w_in.shape[0]
    assert d == D_MODEL and seq % ROW_BLOCK == 0
    n_steps = batch * seq // ROW_BLOCK
    h = x.reshape(batch * seq, d)
    w_in_l = w_in[0].astype(jnp.bfloat16)
    for l in range(depth):
        sgu_b_full = jnp.repeat(sgu_b[l].T, SGU_GROUP_DIM, axis=1)
        post_weights = [_Cast(w, l, batch * seq // PROJ_ROWS)
                        for w in (w_br_att, w_br_sgu, w_out, w_ffn_in, w_ffn_out)]
        qkv, sgu, gate, wa, ws, wo, wi, wf = _proj(
            l, h, norm_mix[l][None], w_in_l, sgu_norm_gain[l][None], sgu_norm_bias[l][None],
            sgu_w, sgu_b_full, b_gate[l].reshape(1, GATE_WIDTH), post_weights)
        att = _attention(qkv, _extended_rel_bias(att_rel_bias[l]), batch, seq)
        last = l == depth - 1
        h, *next_w_in = _post(att, sgu, gate, h, wa, ws, wo, norm_ffn[l][None], wi, wf,
                              norm_final[None], final_norm=last,
                              casts=[] if last else [_Cast(w_in, l + 1, n_steps)])
        if not last:
            w_in_l, = next_w_in
    return h.reshape(batch, seq, d)
```

```python
import functools
import math

import jax
import jax.numpy as jnp
from jax import lax
from jax.experimental import pallas as pl
from jax.experimental.pallas import tpu as pltpu

D_MODEL = 1024
CHUNK = 64
LEFT_CHUNKS = 8
ATT_HEADS = 8
ATT_HEAD_DIM = 64
ATT_WIDTH = ATT_HEADS * ATT_HEAD_DIM
MAX_REL_DIST = 128
SGU_CHUNK = 128
SGU_GROUPS = 4
SGU_WIDTH = 512
SGU_GROUP_DIM = SGU_WIDTH // SGU_GROUPS
N_BRANCHES = 2
GATE_WIDTH = N_BRANCHES * D_MODEL
QKV_WIDTH = 3 * ATT_WIDTH
UV_WIDTH = 2 * SGU_WIDTH
IN_WIDTH = QKV_WIDTH + UV_WIDTH + GATE_WIDTH
D_FF = 2816
EPS = 1e-6
MASK_VALUE = -1e30
LOG2E = math.log2(math.e)
Q_SCALE = ATT_HEAD_DIM ** -0.5 * LOG2E

V7X_VMEM_BYTES = 64 * 1024 * 1024
VMEM_LIMIT_BYTES = V7X_VMEM_BYTES - 4 * 1024 * 1024
LANES = 128

ROW_BLOCK = 512
ATT_QBLOCK = 2 * CHUNK
ATT_WINDOW = (LEFT_CHUNKS + 2) * CHUNK
PAST = LEFT_CHUNKS * CHUNK
EXT_WIDTH = ATT_WINDOW + ATT_QBLOCK


def _params(n_axes, flags=None):
    return pltpu.CompilerParams(
        dimension_semantics=("arbitrary",) * n_axes,
        vmem_limit_bytes=VMEM_LIMIT_BYTES,
        flags=flags)


def _rmsnorm_f32(x, g):
    return x * lax.rsqrt(jnp.mean(x * x, axis=-1, keepdims=True) + EPS) * g


def _gelu_tanh(x):
    c = math.sqrt(2.0 / math.pi)
    return 0.5 * x * (1.0 + jnp.tanh(c * (x + 0.044715 * (x * x * x))))


def _sigmoid(x):
    return 0.5 * jnp.tanh(0.5 * x) + 0.5


def _resident(shape):
    return pl.BlockSpec(shape, lambda *_: (0,) * len(shape), pipeline_mode=pl.Buffered(1))


def _layer_slice(layer, shape):
    return pl.BlockSpec((None,) + shape, lambda *_: (layer,) + (0,) * len(shape),
                        pipeline_mode=pl.Buffered(1))


BF16_SUBLANES = 16


class _Cast:
    def __init__(self, stacked, layer, n_steps):
        _, rows, cols = stacked.shape
        visits = 1
        while (rows * visits) % (n_steps * BF16_SUBLANES):
            visits *= 2
        block_rows = rows * visits // n_steps
        block = lambda i: jnp.minimum(i, n_steps - 1) // visits
        self.operand = stacked
        self.in_spec = pl.BlockSpec((None, block_rows, cols), lambda i: (layer, block(i), 0))
        self.out_spec = pl.BlockSpec((block_rows, cols), lambda i: (block(i), 0))
        self.out_shape = jax.ShapeDtypeStruct((rows, cols), jnp.bfloat16)


def _run_casts(srcs, dsts):
    for src, dst in zip(srcs, dsts, strict=True):
        dst[...] = src[...].astype(dst.dtype)


PROJ_COLS = 512
PROJ_ROWS = 512


def _proj_kernel(x_ref, g_ref, w_ref, lng_ref, lnb_ref, sw_ref, sb_ref, bg_ref, *refs,
                 n_casts, n_blocks):
    cast_srcs = refs[:n_casts]
    qkv_ref, sgu_ref, gate_ref = refs[n_casts:n_casts + 3]
    cast_dsts = refs[n_casts + 3:2 * n_casts + 3]
    uv_scr = refs[2 * n_casts + 3]
    step = pl.program_id(0)
    slot = step % 2

    def gating_unit(slot):
        gv = _gelu_tanh(uv_scr[slot, :, SGU_WIDTH:UV_WIDTH])
        mu = jnp.mean(gv, axis=-1, keepdims=True)
        dv = gv - mu
        var = jnp.mean(dv * dv, axis=-1, keepdims=True)
        vn = (dv * lax.rsqrt(var + EPS) * lng_ref[...] + lnb_ref[...]).astype(jnp.bfloat16)
        gu = _gelu_tanh(uv_scr[slot, :, 0:SGU_WIDTH])
        t_to = lax.broadcasted_iota(jnp.int32, (SGU_CHUNK, SGU_CHUNK), 0)
        t_from = lax.broadcasted_iota(jnp.int32, (SGU_CHUNK, SGU_CHUNK), 1)
        for g in range(SGU_GROUPS):
            cols = slice(g * SGU_GROUP_DIM, (g + 1) * SGU_GROUP_DIM)
            wg = jnp.where(t_to >= t_from, sw_ref[g], 0.0).astype(jnp.bfloat16)
            chunks = [slice(c * SGU_CHUNK, (c + 1) * SGU_CHUNK) for c in range(PROJ_ROWS // SGU_CHUNK)]
            mixed_all = jnp.dot(wg, jnp.concatenate([vn[rows, cols] for rows in chunks], axis=1),
                                preferred_element_type=jnp.float32)
            for c, rows in enumerate(chunks):
                mixed = mixed_all[:, c * SGU_GROUP_DIM:(c + 1) * SGU_GROUP_DIM] + sb_ref[:, cols]
                sgu_ref[rows, cols] = (gu[rows, cols] * mixed).astype(sgu_ref.dtype)

    @pl.when(step == 0)
    def _():
        uv_scr[1] = jnp.zeros(uv_scr.shape[1:], uv_scr.dtype)

    @pl.when(step < n_blocks)
    def _():
        _run_casts(cast_srcs, cast_dsts)
        xn = _rmsnorm_f32(x_ref[...], g_ref[...]).astype(jnp.bfloat16)

        def project(start):
            return jnp.dot(xn, w_ref[:, start:start + PROJ_COLS], preferred_element_type=jnp.float32)

        uv_scr[slot, :, 0:SGU_WIDTH] = project(QKV_WIDTH)
        uv_scr[slot, :, SGU_WIDTH:UV_WIDTH] = project(QKV_WIDTH + SGU_WIDTH)
        for c in range(0, GATE_WIDTH, PROJ_COLS):
            logits = project(QKV_WIDTH + UV_WIDTH + c) + bg_ref[:, c:c + PROJ_COLS]
            gate_ref[:, c:c + PROJ_COLS] = _sigmoid(logits).astype(gate_ref.dtype)
        assert PROJ_COLS == ATT_WIDTH
        qkv_ref[:, 0:ATT_WIDTH] = (project(0) * Q_SCALE).astype(qkv_ref.dtype)
        gating_unit(1 - slot)
        for c in range(ATT_WIDTH, QKV_WIDTH, PROJ_COLS):
            qkv_ref[:, c:c + PROJ_COLS] = project(c).astype(qkv_ref.dtype)

    @pl.when(step == n_blocks)
    def _():
        gating_unit(1 - slot)


def _proj(layer, h, g, w, ln_g, ln_b, sgu_w, sgu_b_full, b_gate, casts):
    t = h.shape[0]
    n_blocks = t // PROJ_ROWS
    cur = lambda i: (jnp.minimum(i, n_blocks - 1), 0)
    prev = lambda i: (jnp.maximum(i - 1, 0), 0)
    return pl.pallas_call(
        functools.partial(_proj_kernel, n_casts=len(casts), n_blocks=n_blocks),
        grid=(n_blocks + 1,),
        in_specs=[pl.BlockSpec((PROJ_ROWS, D_MODEL), cur),
                  _resident((1, D_MODEL)),
                  _resident((D_MODEL, IN_WIDTH)),
                  _resident((1, SGU_WIDTH)), _resident((1, SGU_WIDTH)),
                  _layer_slice(layer, (SGU_GROUPS, SGU_CHUNK, SGU_CHUNK)),
                  _resident((SGU_CHUNK, SGU_WIDTH)),
                  _resident((1, GATE_WIDTH))] + [c.in_spec for c in casts],
        out_specs=[pl.BlockSpec((PROJ_ROWS, QKV_WIDTH), cur),
                   pl.BlockSpec((PROJ_ROWS, SGU_WIDTH), prev),
                   pl.BlockSpec((PROJ_ROWS, GATE_WIDTH), cur)] + [c.out_spec for c in casts],
        out_shape=[jax.ShapeDtypeStruct((t, QKV_WIDTH), jnp.bfloat16),
                   jax.ShapeDtypeStruct((t, SGU_WIDTH), jnp.bfloat16),
                   jax.ShapeDtypeStruct((t, GATE_WIDTH), jnp.bfloat16)] + [c.out_shape for c in casts],
        scratch_shapes=[pltpu.VMEM((2, PROJ_ROWS, UV_WIDTH), jnp.float32)],
        compiler_params=_params(1),
        name="proj",
    )(h, g, w, ln_g, ln_b, sgu_w, sgu_b_full, b_gate, *[c.operand for c in casts])


SOFTMAX_ROWS = 16
N_SUB = ROW_BLOCK // ATT_QBLOCK
ATT_UNITS = N_SUB * (ATT_HEADS // 2)


def _expand_bias(ext_ref, bias_scr):
    r = lax.broadcasted_iota(jnp.int32, (ATT_QBLOCK, ATT_WINDOW), 0)
    c = lax.broadcasted_iota(jnp.int32, (ATT_QBLOCK, ATT_WINDOW), 1)
    qc, kc = r // CHUNK, c // CHUNK
    visible = (kc >= qc) & (kc <= qc + LEFT_CHUNKS)
    for h in range(ATT_HEADS):
        rows = jnp.broadcast_to(ext_ref[h], (ATT_QBLOCK, EXT_WIDTH))
        toeplitz = pltpu.roll(rows, EXT_WIDTH - (ATT_QBLOCK - 1), 1, stride=1, stride_axis=0)
        bias_scr[h] = jnp.where(visible, toeplitz[:, 0:ATT_WINDOW] * LOG2E, MASK_VALUE)


def _fill_windows(kp_ref, kc_ref, vp_ref, vc_ref, first_block, kwin_lo, kwin_hi, vwin_lo, vwin_hi):
    col = lax.broadcasted_iota(jnp.int32, (ROW_BLOCK, ATT_WIDTH), 1)
    low_cols = (col % LANES) < ATT_HEAD_DIM
    missing = jnp.where(first_block, MASK_VALUE, 0.0).astype(jnp.bfloat16)
    for k_ref, v_ref, rows, key_term in (
            (kp_ref, vp_ref, slice(0, ROW_BLOCK), missing),
            (kc_ref, vc_ref, slice(ROW_BLOCK, 2 * ROW_BLOCK), jnp.bfloat16(0.0))):
        k, v = k_ref[...], v_ref[...]
        kwin_lo[rows, :] = jnp.where(low_cols, k, key_term)
        kwin_hi[rows, :] = jnp.where(low_cols, key_term, k)
        vwin_lo[rows, :] = jnp.where(low_cols, v, jnp.ones_like(v))
        vwin_hi[rows, :] = jnp.where(low_cols, jnp.ones_like(v), v)


def _attention_phases(q_ref, write_out, kwin_lo, kwin_hi, vwin_lo, vwin_hi, bias_scr, s_scr, e_scr):
    lane = lax.broadcasted_iota(jnp.int32, (ATT_QBLOCK, LANES), 1)
    low_half = lane < ATT_HEAD_DIM
    pick_hi_lane = jnp.where(lane == ATT_HEAD_DIM, 1.0, 0.0).astype(jnp.bfloat16)
    pick_lo_lane = jnp.where(lane == 0, 1.0, 0.0).astype(jnp.bfloat16)

    def where(u):
        j, hp = divmod(u, ATT_HEADS // 2)
        queries = slice(j * ATT_QBLOCK, (j + 1) * ATT_QBLOCK)
        window = slice(j * ATT_QBLOCK, j * ATT_QBLOCK + ATT_WINDOW)
        cols = slice(hp * LANES, (hp + 1) * LANES)
        return u % 2, hp, queries, window, cols

    def scores(u):
        slot, _, queries, window, cols = where(u)
        qp = q_ref[queries, cols]
        q_lo = jnp.where(low_half, qp, pick_hi_lane)
        q_hi = jnp.where(low_half, pick_lo_lane, qp)
        s_scr[slot, 0] = lax.dot_general(q_lo, kwin_lo[window, cols], (((1,), (1,)), ((), ())),
                                         preferred_element_type=jnp.float32)
        s_scr[slot, 1] = lax.dot_general(q_hi, kwin_hi[window, cols], (((1,), (1,)), ((), ())),
                                         preferred_element_type=jnp.float32)

    def softmax(u):
        slot, hp, _, _, _ = where(u)
        for half in range(2):
            for g in range(ATT_QBLOCK // SOFTMAX_ROWS):
                rows = slice(g * SOFTMAX_ROWS, (g + 1) * SOFTMAX_ROWS)
                s = s_scr[slot, half, rows, :] + bias_scr[2 * hp + half, rows, :]
                e = jnp.exp2(s - jnp.max(s, axis=-1, keepdims=True))
                e_scr[slot, half, rows, :] = e.astype(jnp.bfloat16)

    def values(u):
        slot, _, queries, window, cols = where(u)
        o_lo = jnp.dot(e_scr[slot, 0], vwin_lo[window, cols], preferred_element_type=jnp.float32)
        o_hi = jnp.dot(e_scr[slot, 1], vwin_hi[window, cols], preferred_element_type=jnp.float32)
        denom = pltpu.roll(jnp.where(low_half, o_hi, o_lo), ATT_HEAD_DIM, 1)
        o = jnp.where(low_half, o_lo, o_hi) * (1.0 / denom)
        write_out(queries, cols, o.astype(jnp.bfloat16))

    return scores, softmax, values


def _extended_rel_bias(rel_bias):
    d_first = PAST + ATT_QBLOCK - 1
    d_last = d_first - (EXT_WIDTH - 1)
    assert d_first >= MAX_REL_DIST and d_last == -MAX_REL_DIST
    n_clipped = d_first - MAX_REL_DIST
    table = rel_bias.astype(jnp.float32)
    top = jnp.broadcast_to(table[:, 2 * MAX_REL_DIST:], (ATT_HEADS, n_clipped))
    return jnp.concatenate([top, table[:, ::-1]], axis=1)[:, None, :]


MXU_TILE = 256


def _post_kernel(qn_ref, kp_ref, kc_ref, vp_ref, vc_ref, q0_ref, k0_ref, v0_ref, ext_ref,
                 sgu_ref, gate_ref, h_ref, wa_ref, ws_ref, wo_ref, g_ref, wi_ref, wf_ref, gf_ref,
                 o_ref, kwin_lo, kwin_hi, vwin_lo, vwin_hi, bias_scr, s_scr, e_scr, att_scr, act_scr,
                 *, final_norm, n_blocks, blocks_per_seq):
    windows = (kwin_lo, kwin_hi, vwin_lo, vwin_hi)
    step = pl.program_id(0)

    def attention_into(slot, q_ref, kp, kc, vp, vc, first_block):
        _fill_windows(kp, kc, vp, vc, first_block, *windows)

        def write_out(rows, cols, value):
            att_scr[slot, rows, cols] = value

        return _attention_phases(q_ref, write_out, *windows, bias_scr, s_scr, e_scr)

    @pl.when(step == 0)
    def _():
        _expand_bias(ext_ref, bias_scr)
        scores, softmax, values = attention_into(0, q0_ref, k0_ref, k0_ref, v0_ref, v0_ref, True)
        scores(0)
        for u in range(ATT_UNITS):
            if u + 1 < ATT_UNITS:
                scores(u + 1)
            softmax(u)
            values(u)

    next_block = jnp.minimum(step + 1, n_blocks - 1)
    scores, softmax, values = attention_into(
        (step + 1) % 2, qn_ref, kp_ref, kc_ref, vp_ref, vc_ref, next_block % blocks_per_seq == 0)
    units_done = [0]

    def attention_unit():
        u = units_done[0]
        values(u)
        if u + 1 < ATT_UNITS:
            scores(u + 1)
            softmax(u + 1)
        units_done[0] = u + 1

    scores(0)
    softmax(0)

    br_att = jnp.dot(att_scr[step % 2], wa_ref[...], preferred_element_type=jnp.float32)
    attention_unit()
    br_sgu = jnp.dot(sgu_ref[...], ws_ref[...], preferred_element_type=jnp.float32)
    attention_unit()
    g_att = gate_ref[:, 0:D_MODEL].astype(jnp.float32)
    g_sgu = gate_ref[:, D_MODEL:GATE_WIDTH].astype(jnp.float32)
    merged = (g_att * br_att + g_sgu * br_sgu).astype(jnp.bfloat16)
    h = h_ref[...] + jnp.dot(merged, wo_ref[...], preferred_element_type=jnp.float32)
    attention_unit()

    hn = _rmsnorm_f32(h, g_ref[...]).astype(jnp.bfloat16)
    for start in range(0, D_FF, MXU_TILE):
        cols = slice(start, start + MXU_TILE)
        up_cols = slice(D_FF + start, D_FF + start + MXU_TILE)
        gate = jnp.dot(hn, wi_ref[:, cols], preferred_element_type=jnp.float32)
        up = jnp.dot(hn, wi_ref[:, up_cols], preferred_element_type=jnp.float32)
        act_scr[:, cols] = (gate * _sigmoid(gate) * up).astype(jnp.bfloat16)
        attention_unit()
    down = []
    for start in range(0, D_MODEL, MXU_TILE):
        down.append(jnp.dot(act_scr[...], wf_ref[:, start:start + MXU_TILE],
                            preferred_element_type=jnp.float32))
        if units_done[0] < ATT_UNITS:
            attention_unit()
    assert units_done[0] == ATT_UNITS
    out = h + jnp.concatenate(down, axis=1)
    if final_norm:
        out = _rmsnorm_f32(out, gf_ref[...])
    o_ref[...] = out


def _post(qkv, ext, sgu, gate, h, wa, ws, wo, g, wi, wf, g_final, final_norm, blocks_per_seq):
    t = h.shape[0]
    n_blocks = t // ROW_BLOCK
    row = lambda i: (i, 0)
    nxt = lambda i: jnp.minimum(i + 1, n_blocks - 1)
    before_nxt = lambda i: nxt(i) - (nxt(i) % blocks_per_seq != 0).astype(jnp.int32)
    blk = (ROW_BLOCK, ATT_WIDTH)
    first = lambda col: pl.BlockSpec(blk, lambda i: (0, col), pipeline_mode=pl.Buffered(1))
    window = (2 * ROW_BLOCK, ATT_WIDTH)
    return pl.pallas_call(
        functools.partial(_post_kernel, final_norm=final_norm, n_blocks=n_blocks,
                          blocks_per_seq=blocks_per_seq),
        grid=(n_blocks,),
        in_specs=[pl.BlockSpec(blk, lambda i: (nxt(i), 0)),
                  pl.BlockSpec(blk, lambda i: (before_nxt(i), 1)),
                  pl.BlockSpec(blk, lambda i: (nxt(i), 1)),
                  pl.BlockSpec(blk, lambda i: (before_nxt(i), 2)),
                  pl.BlockSpec(blk, lambda i: (nxt(i), 2)),
                  first(0), first(1), first(2),
                  _resident((ATT_HEADS, 1, EXT_WIDTH)),
                  pl.BlockSpec((ROW_BLOCK, SGU_WIDTH), row),
                  pl.BlockSpec((ROW_BLOCK, GATE_WIDTH), row),
                  pl.BlockSpec((ROW_BLOCK, D_MODEL), row),
                  _resident((ATT_WIDTH, D_MODEL)),
                  _resident((SGU_WIDTH, D_MODEL)),
                  _resident((D_MODEL, D_MODEL)),
                  _resident((1, D_MODEL)),
                  _resident((D_MODEL, 2 * D_FF)),
                  _resident((D_FF, D_MODEL)),
                  _resident((1, D_MODEL))],
        out_specs=pl.BlockSpec((ROW_BLOCK, D_MODEL), row),
        out_shape=jax.ShapeDtypeStruct((t, D_MODEL), jnp.float32),
        scratch_shapes=[pltpu.VMEM(window, jnp.bfloat16), pltpu.VMEM(window, jnp.bfloat16),
                        pltpu.VMEM(window, jnp.bfloat16), pltpu.VMEM(window, jnp.bfloat16),
                        pltpu.VMEM((ATT_HEADS, ATT_QBLOCK, ATT_WINDOW), jnp.float32),
                        pltpu.VMEM((2, 2, ATT_QBLOCK, ATT_WINDOW), jnp.float32),
                        pltpu.VMEM((2, 2, ATT_QBLOCK, ATT_WINDOW), jnp.bfloat16),
                        pltpu.VMEM((2, ROW_BLOCK, ATT_WIDTH), jnp.bfloat16),
                        pltpu.VMEM((ROW_BLOCK, D_FF), jnp.bfloat16)],
        compiler_params=_params(1),
        name="post_final" if final_norm else "post",
    )(qkv, qkv, qkv, qkv, qkv, qkv, qkv, qkv, ext, sgu, gate, h, wa, ws, wo, g, wi, wf, g_final)


def kernel(x, norm_mix, w_in, att_rel_bias, sgu_norm_gain, sgu_norm_bias, sgu_w, sgu_b, w_br_att, w_br_sgu, b_gate, w_out, norm_ffn, w_ffn_in, w_ffn_out, norm_final):
    batch, seq, d = x.shape
    depth = w_in.shape[0]
    assert d == D_MODEL and seq % ROW_BLOCK == 0
    n_steps = batch * seq // PROJ_ROWS
    h = x.reshape(batch * seq, d)
    w_in_l = w_in[0].astype(jnp.bfloat16)
    for l in range(depth):
        sgu_b_full = jnp.repeat(sgu_b[l].T, SGU_GROUP_DIM, axis=1)
        last = l == depth - 1
        casts = [_Cast(w, l, n_steps) for w in (w_br_att, w_br_sgu, w_out, w_ffn_in, w_ffn_out)]
        if not last:
            casts.append(_Cast(w_in, l + 1, n_steps))
        qkv, sgu, gate, wa, ws, wo, wi, wf, *next_w_in = _proj(
            l, h, norm_mix[l][None], w_in_l, sgu_norm_gain[l][None], sgu_norm_bias[l][None],
            sgu_w, sgu_b_full, b_gate[l].reshape(1, GATE_WIDTH), casts)
        h = _post(qkv, _extended_rel_bias(att_rel_bias[l]), sgu, gate, h, wa, ws, wo,
                  norm_ffn[l][None], wi, wf, norm_final[None], final_norm=last,
                  blocks_per_seq=seq // ROW_BLOCK)
        if not last:
            w_in_l, = next_w_in
    return h.reshape(batch, seq, d)
```

```python
import functools
import math

import jax
import jax.numpy as jnp
from jax import lax
from jax.experimental import pallas as pl
from jax.experimental.pallas import tpu as pltpu

D_MODEL = 1024
CHUNK = 64
LEFT_CHUNKS = 8
ATT_HEADS = 8
ATT_HEAD_DIM = 64
ATT_WIDTH = ATT_HEADS * ATT_HEAD_DIM
MAX_REL_DIST = 128
SGU_CHUNK = 128
SGU_GROUPS = 4
SGU_WIDTH = 512
SGU_GROUP_DIM = SGU_WIDTH // SGU_GROUPS
N_BRANCHES = 2
GATE_WIDTH = N_BRANCHES * D_MODEL
QKV_WIDTH = 3 * ATT_WIDTH
UV_WIDTH = 2 * SGU_WIDTH
IN_WIDTH = QKV_WIDTH + UV_WIDTH + GATE_WIDTH
D_FF = 2816
EPS = 1e-6
MASK_VALUE = -1e30
LOG2E = math.log2(math.e)
Q_SCALE = ATT_HEAD_DIM ** -0.5 * LOG2E

V7X_VMEM_BYTES = 64 * 1024 * 1024
VMEM_LIMIT_BYTES = V7X_VMEM_BYTES - 4 * 1024 * 1024
LANES = 128

ROW_BLOCK = 512
ATT_QBLOCK = 2 * CHUNK
ATT_WINDOW = (LEFT_CHUNKS + 2) * CHUNK
PAST = LEFT_CHUNKS * CHUNK
EXT_WIDTH = ATT_WINDOW + ATT_QBLOCK


def _params(n_axes, flags=None):
    return pltpu.CompilerParams(
        dimension_semantics=("arbitrary",) * n_axes,
        vmem_limit_bytes=VMEM_LIMIT_BYTES,
        flags=flags)


def _rmsnorm_f32(x, g):
    return x * lax.rsqrt(jnp.mean(x * x, axis=-1, keepdims=True) + EPS) * g


def _gelu_tanh(x):
    c = math.sqrt(2.0 / math.pi)
    return 0.5 * x * (1.0 + jnp.tanh(c * (x + 0.044715 * (x * x * x))))


def _sigmoid(x):
    return 0.5 * jnp.tanh(0.5 * x) + 0.5


def _resident(shape):
    return pl.BlockSpec(shape, lambda *_: (0,) * len(shape), pipeline_mode=pl.Buffered(1))


def _layer_slice(layer, shape):
    return pl.BlockSpec((None,) + shape, lambda *_: (layer,) + (0,) * len(shape),
                        pipeline_mode=pl.Buffered(1))


BF16_SUBLANES = 16


class _Cast:
    def __init__(self, stacked, layer, n_steps):
        _, rows, cols = stacked.shape
        visits = 1
        while (rows * visits) % (n_steps * BF16_SUBLANES):
            visits *= 2
        block_rows = rows * visits // n_steps
        block = lambda i: jnp.minimum(i, n_steps - 1) // visits
        self.operand = stacked
        self.in_spec = pl.BlockSpec((None, block_rows, cols), lambda i: (layer, block(i), 0))
        self.out_spec = pl.BlockSpec((block_rows, cols), lambda i: (block(i), 0))
        self.out_shape = jax.ShapeDtypeStruct((rows, cols), jnp.bfloat16)


def _run_casts(srcs, dsts):
    for src, dst in zip(srcs, dsts, strict=True):
        dst[...] = src[...].astype(dst.dtype)


PROJ_COLS = 512
PROJ_ROWS = 512


def _proj_kernel(x_ref, g_ref, w_ref, lng_ref, lnb_ref, sw_ref, sb_ref, bg_ref, *refs,
                 n_casts, n_blocks):
    cast_srcs = refs[:n_casts]
    qkv_ref, sgu_ref, gate_ref = refs[n_casts:n_casts + 3]
    cast_dsts = refs[n_casts + 3:2 * n_casts + 3]
    uv_scr = refs[2 * n_casts + 3]
    step = pl.program_id(0)
    slot = step % 2

    def gating_unit(slot):
        gv = _gelu_tanh(uv_scr[slot, :, SGU_WIDTH:UV_WIDTH])
        mu = jnp.mean(gv, axis=-1, keepdims=True)
        dv = gv - mu
        var = jnp.mean(dv * dv, axis=-1, keepdims=True)
        vn = (dv * lax.rsqrt(var + EPS) * lng_ref[...] + lnb_ref[...]).astype(jnp.bfloat16)
        gu = _gelu_tanh(uv_scr[slot, :, 0:SGU_WIDTH])
        t_to = lax.broadcasted_iota(jnp.int32, (SGU_CHUNK, SGU_CHUNK), 0)
        t_from = lax.broadcasted_iota(jnp.int32, (SGU_CHUNK, SGU_CHUNK), 1)
        for g in range(SGU_GROUPS):
            cols = slice(g * SGU_GROUP_DIM, (g + 1) * SGU_GROUP_DIM)
            wg = jnp.where(t_to >= t_from, sw_ref[g], 0.0).astype(jnp.bfloat16)
            chunks = [slice(c * SGU_CHUNK, (c + 1) * SGU_CHUNK) for c in range(PROJ_ROWS // SGU_CHUNK)]
            mixed_all = jnp.dot(wg, jnp.concatenate([vn[rows, cols] for rows in chunks], axis=1),
                                preferred_element_type=jnp.float32)
            for c, rows in enumerate(chunks):
                mixed = mixed_all[:, c * SGU_GROUP_DIM:(c + 1) * SGU_GROUP_DIM] + sb_ref[:, cols]
                sgu_ref[rows, cols] = (gu[rows, cols] * mixed).astype(sgu_ref.dtype)

    @pl.when(step == 0)
    def _():
        uv_scr[1] = jnp.zeros(uv_scr.shape[1:], uv_scr.dtype)

    @pl.when(step < n_blocks)
    def _():
        _run_casts(cast_srcs, cast_dsts)
        xn = _rmsnorm_f32(x_ref[...], g_ref[...]).astype(jnp.bfloat16)

        def project(start):
            return jnp.dot(xn, w_ref[:, start:start + PROJ_COLS], preferred_element_type=jnp.float32)

        uv_scr[slot, :, 0:SGU_WIDTH] = project(QKV_WIDTH)
        uv_scr[slot, :, SGU_WIDTH:UV_WIDTH] = project(QKV_WIDTH + SGU_WIDTH)
        for c in range(0, GATE_WIDTH, PROJ_COLS):
            logits = project(QKV_WIDTH + UV_WIDTH + c) + bg_ref[:, c:c + PROJ_COLS]
            gate_ref[:, c:c + PROJ_COLS] = _sigmoid(logits).astype(gate_ref.dtype)
        assert PROJ_COLS == ATT_WIDTH
        qkv_ref[:, 0:ATT_WIDTH] = (project(0) * Q_SCALE).astype(qkv_ref.dtype)
        gating_unit(1 - slot)
        for c in range(ATT_WIDTH, QKV_WIDTH, PROJ_COLS):
            qkv_ref[:, c:c + PROJ_COLS] = project(c).astype(qkv_ref.dtype)

    @pl.when(step == n_blocks)
    def _():
        gating_unit(1 - slot)


def _proj(layer, h, g, w, ln_g, ln_b, sgu_w, sgu_b_full, b_gate, casts):
    t = h.shape[0]
    n_blocks = t // PROJ_ROWS
    cur = lambda i: (jnp.minimum(i, n_blocks - 1), 0)
    prev = lambda i: (jnp.maximum(i - 1, 0), 0)
    return pl.pallas_call(
        functools.partial(_proj_kernel, n_casts=len(casts), n_blocks=n_blocks),
        grid=(n_blocks + 1,),
        in_specs=[pl.BlockSpec((PROJ_ROWS, D_MODEL), cur),
                  _resident((1, D_MODEL)),
                  _resident((D_MODEL, IN_WIDTH)),
                  _resident((1, SGU_WIDTH)), _resident((1, SGU_WIDTH)),
                  _layer_slice(layer, (SGU_GROUPS, SGU_CHUNK, SGU_CHUNK)),
                  _resident((SGU_CHUNK, SGU_WIDTH)),
                  _resident((1, GATE_WIDTH))] + [c.in_spec for c in casts],
        out_specs=[pl.BlockSpec((PROJ_ROWS, QKV_WIDTH), cur),
                   pl.BlockSpec((PROJ_ROWS, SGU_WIDTH), prev),
                   pl.BlockSpec((PROJ_ROWS, GATE_WIDTH), cur)] + [c.out_spec for c in casts],
        out_shape=[jax.ShapeDtypeStruct((t, QKV_WIDTH), jnp.bfloat16),
                   jax.ShapeDtypeStruct((t, SGU_WIDTH), jnp.bfloat16),
                   jax.ShapeDtypeStruct((t, GATE_WIDTH), jnp.bfloat16)] + [c.out_shape for c in casts],
        scratch_shapes=[pltpu.VMEM((2, PROJ_ROWS, UV_WIDTH), jnp.float32)],
        compiler_params=_params(1),
        name="proj",
    )(h, g, w, ln_g, ln_b, sgu_w, sgu_b_full, b_gate, *[c.operand for c in casts])


SOFTMAX_ROWS = 16
N_SUB = ROW_BLOCK // ATT_QBLOCK
ATT_UNITS = N_SUB * (ATT_HEADS // 2)


def _expand_bias(ext_ref, bias_scr):
    r = lax.broadcasted_iota(jnp.int32, (ATT_QBLOCK, ATT_WINDOW), 0)
    c = lax.broadcasted_iota(jnp.int32, (ATT_QBLOCK, ATT_WINDOW), 1)
    qc, kc = r // CHUNK, c // CHUNK
    visible = (kc >= qc) & (kc <= qc + LEFT_CHUNKS)
    for h in range(ATT_HEADS):
        rows = jnp.broadcast_to(ext_ref[h], (ATT_QBLOCK, EXT_WIDTH))
        toeplitz = pltpu.roll(rows, EXT_WIDTH - (ATT_QBLOCK - 1), 1, stride=1, stride_axis=0)
        bias_scr[h] = jnp.where(visible, toeplitz[:, 0:ATT_WINDOW] * LOG2E, MASK_VALUE)


def _fill_windows(kp_ref, kc_ref, vp_ref, vc_ref, first_block, kwin_lo, kwin_hi, vwin_lo, vwin_hi):
    col = lax.broadcasted_iota(jnp.int32, (ROW_BLOCK, ATT_WIDTH), 1)
    low_cols = (col % LANES) < ATT_HEAD_DIM
    missing = jnp.where(first_block, MASK_VALUE, 0.0).astype(jnp.bfloat16)
    for k_ref, v_ref, rows, key_term in (
            (kp_ref, vp_ref, slice(0, ROW_BLOCK), missing),
            (kc_ref, vc_ref, slice(ROW_BLOCK, 2 * ROW_BLOCK), jnp.bfloat16(0.0))):
        k, v = k_ref[...], v_ref[...]
        kwin_lo[rows, :] = jnp.where(low_cols, k, key_term)
        kwin_hi[rows, :] = jnp.where(low_cols, key_term, k)
        vwin_lo[rows, :] = jnp.where(low_cols, v, jnp.ones_like(v))
        vwin_hi[rows, :] = jnp.where(low_cols, jnp.ones_like(v), v)


def _attention_phases(q_ref, write_out, kwin_lo, kwin_hi, vwin_lo, vwin_hi, bias_scr, s_scr, e_scr):
    lane = lax.broadcasted_iota(jnp.int32, (ATT_QBLOCK, LANES), 1)
    low_half = lane < ATT_HEAD_DIM
    pick_hi_lane = jnp.where(lane == ATT_HEAD_DIM, 1.0, 0.0).astype(jnp.bfloat16)
    pick_lo_lane = jnp.where(lane == 0, 1.0, 0.0).astype(jnp.bfloat16)

    def where(u):
        j, hp = divmod(u, ATT_HEADS // 2)
        queries = slice(j * ATT_QBLOCK, (j + 1) * ATT_QBLOCK)
        window = slice(j * ATT_QBLOCK, j * ATT_QBLOCK + ATT_WINDOW)
        cols = slice(hp * LANES, (hp + 1) * LANES)
        return u % 2, hp, queries, window, cols

    def scores(u):
        slot, _, queries, window, cols = where(u)
        qp = q_ref[queries, cols]
        q_lo = jnp.where(low_half, qp, pick_hi_lane)
        q_hi = jnp.where(low_half, pick_lo_lane, qp)
        s_scr[slot, 0] = lax.dot_general(q_lo, kwin_lo[window, cols], (((1,), (1,)), ((), ())),
                                         preferred_element_type=jnp.float32)
        s_scr[slot, 1] = lax.dot_general(q_hi, kwin_hi[window, cols], (((1,), (1,)), ((), ())),
                                         preferred_element_type=jnp.float32)

    def softmax(u):
        slot, hp, _, _, _ = where(u)
        for half in range(2):
            for g in range(ATT_QBLOCK // SOFTMAX_ROWS):
                rows = slice(g * SOFTMAX_ROWS, (g + 1) * SOFTMAX_ROWS)
                s = s_scr[slot, half, rows, :] + bias_scr[2 * hp + half, rows, :]
                e = jnp.exp2(s - jnp.max(s, axis=-1, keepdims=True))
                e_scr[slot, half, rows, :] = e.astype(jnp.bfloat16)

    def values(u):
        slot, _, queries, window, cols = where(u)
        o_lo = jnp.dot(e_scr[slot, 0], vwin_lo[window, cols], preferred_element_type=jnp.float32)
        o_hi = jnp.dot(e_scr[slot, 1], vwin_hi[window, cols], preferred_element_type=jnp.float32)
        denom = pltpu.roll(jnp.where(low_half, o_hi, o_lo), ATT_HEAD_DIM, 1)
        o = jnp.where(low_half, o_lo, o_hi) * (1.0 / denom)
        write_out(queries, cols, o.astype(jnp.bfloat16))

    return scores, softmax, values


def _extended_rel_bias(rel_bias):
    d_first = PAST + ATT_QBLOCK - 1
    d_last = d_first - (EXT_WIDTH - 1)
    assert d_first >= MAX_REL_DIST and d_last == -MAX_REL_DIST
    n_clipped = d_first - MAX_REL_DIST
    table = rel_bias.astype(jnp.float32)
    top = jnp.broadcast_to(table[:, 2 * MAX_REL_DIST:], (ATT_HEADS, n_clipped))
    return jnp.concatenate([top, table[:, ::-1]], axis=1)[:, None, :]


MXU_TILE = 256
N_MXU = 2


def _post_kernel(qn_ref, kp_ref, kc_ref, vp_ref, vc_ref, q0_ref, k0_ref, v0_ref, ext_ref,
                 sgu_ref, gate_ref, h_ref, wa_ref, ws_ref, wo_ref, g_ref, wi_ref, wf_ref, gf_ref,
                 o_ref, kwin_lo, kwin_hi, vwin_lo, vwin_hi, bias_scr, s_scr, e_scr, att_scr, act_scr,
                 *, final_norm, n_blocks, blocks_per_seq):
    windows = (kwin_lo, kwin_hi, vwin_lo, vwin_hi)
    step = pl.program_id(0)

    def attention_into(slot, q_ref, kp, kc, vp, vc, first_block):
        _fill_windows(kp, kc, vp, vc, first_block, *windows)

        def write_out(rows, cols, value):
            att_scr[slot, rows, cols] = value

        return _attention_phases(q_ref, write_out, *windows, bias_scr, s_scr, e_scr)

    @pl.when(step == 0)
    def _():
        _expand_bias(ext_ref, bias_scr)
        scores, softmax, values = attention_into(0, q0_ref, k0_ref, k0_ref, v0_ref, v0_ref, True)
        scores(0)
        for u in range(ATT_UNITS):
            if u + 1 < ATT_UNITS:
                scores(u + 1)
            softmax(u)
            values(u)

    next_block = jnp.minimum(step + 1, n_blocks - 1)
    scores, softmax, values = attention_into(
        (step + 1) % 2, qn_ref, kp_ref, kc_ref, vp_ref, vc_ref, next_block % blocks_per_seq == 0)
    units_done = [0]

    def attention_unit():
        u = units_done[0]
        values(u)
        if u + 1 < ATT_UNITS:
            scores(u + 1)
            softmax(u + 1)
        units_done[0] = u + 1

    scores(0)
    softmax(0)

    br_att = jnp.dot(att_scr[step % 2], wa_ref[...], preferred_element_type=jnp.float32)
    attention_unit()
    br_sgu = jnp.dot(sgu_ref[...], ws_ref[...], preferred_element_type=jnp.float32)
    attention_unit()
    g_att = gate_ref[:, 0:D_MODEL].astype(jnp.float32)
    g_sgu = gate_ref[:, D_MODEL:GATE_WIDTH].astype(jnp.float32)
    merged = (g_att * br_att + g_sgu * br_sgu).astype(jnp.bfloat16)
    h = h_ref[...] + jnp.dot(merged, wo_ref[...], preferred_element_type=jnp.float32)
    attention_unit()

    hn = _rmsnorm_f32(h, g_ref[...]).astype(jnp.bfloat16)
    for start in range(0, D_FF, MXU_TILE):
        cols = slice(start, start + MXU_TILE)
        up_cols = slice(D_FF + start, D_FF + start + MXU_TILE)
        gate = jnp.dot(hn, wi_ref[:, cols], preferred_element_type=jnp.float32)
        up = jnp.dot(hn, wi_ref[:, up_cols], preferred_element_type=jnp.float32)
        act_scr[:, cols] = (gate * _sigmoid(gate) * up).astype(jnp.bfloat16)
        attention_unit()
    down = []
    for start in range(0, D_MODEL, N_MXU * MXU_TILE):
        down.append(jnp.dot(act_scr[...], wf_ref[:, start:start + N_MXU * MXU_TILE],
                            preferred_element_type=jnp.float32))
        if units_done[0] < ATT_UNITS:
            attention_unit()
    assert units_done[0] == ATT_UNITS
    out = h + jnp.concatenate(down, axis=1)
    if final_norm:
        out = _rmsnorm_f32(out, gf_ref[...])
    o_ref[...] = out


def _post(qkv, ext, sgu, gate, h, wa, ws, wo, g, wi, wf, g_final, final_norm, blocks_per_seq):
    t = h.shape[0]
    n_blocks = t // ROW_BLOCK
    row = lambda i: (i, 0)
    nxt = lambda i: jnp.minimum(i + 1, n_blocks - 1)
    before_nxt = lambda i: nxt(i) - (nxt(i) % blocks_per_seq != 0).astype(jnp.int32)
    blk = (ROW_BLOCK, ATT_WIDTH)
    first = lambda col: pl.BlockSpec(blk, lambda i: (0, col), pipeline_mode=pl.Buffered(1))
    window = (2 * ROW_BLOCK, ATT_WIDTH)
    return pl.pallas_call(
        functools.partial(_post_kernel, final_norm=final_norm, n_blocks=n_blocks,
                          blocks_per_seq=blocks_per_seq),
        grid=(n_blocks,),
        in_specs=[pl.BlockSpec(blk, lambda i: (nxt(i), 0)),
                  pl.BlockSpec(blk, lambda i: (before_nxt(i), 1)),
                  pl.BlockSpec(blk, lambda i: (nxt(i), 1)),
                  pl.BlockSpec(blk, lambda i: (before_nxt(i), 2)),
                  pl.BlockSpec(blk, lambda i: (nxt(i), 2)),
                  first(0), first(1), first(2),
                  _resident((ATT_HEADS, 1, EXT_WIDTH)),
                  pl.BlockSpec((ROW_BLOCK, SGU_WIDTH), row),
                  pl.BlockSpec((ROW_BLOCK, GATE_WIDTH), row),
                  pl.BlockSpec((ROW_BLOCK, D_MODEL), row),
                  _resident((ATT_WIDTH, D_MODEL)),
                  _resident((SGU_WIDTH, D_MODEL)),
                  _resident((D_MODEL, D_MODEL)),
                  _resident((1, D_MODEL)),
                  _resident((D_MODEL, 2 * D_FF)),
                  _resident((D_FF, D_MODEL)),
                  _resident((1, D_MODEL))],
        out_specs=pl.BlockSpec((ROW_BLOCK, D_MODEL), row),
        out_shape=jax.ShapeDtypeStruct((t, D_MODEL), jnp.float32),
        scratch_shapes=[pltpu.VMEM(window, jnp.bfloat16), pltpu.VMEM(window, jnp.bfloat16),
                        pltpu.VMEM(window, jnp.bfloat16), pltpu.VMEM(window, jnp.bfloat16),
                        pltpu.VMEM((ATT_HEADS, ATT_QBLOCK, ATT_WINDOW), jnp.float32),
                        pltpu.VMEM((2, 2, ATT_QBLOCK, ATT_WINDOW), jnp.float32),
                        pltpu.VMEM((2, 2, ATT_QBLOCK, ATT_WINDOW), jnp.bfloat16),
                        pltpu.VMEM((2, ROW_BLOCK, ATT_WIDTH), jnp.bfloat16),
                        pltpu.VMEM((ROW_BLOCK, D_FF), jnp.bfloat16)],
        compiler_params=_params(1),
        name="post_final" if final_norm else "post",
    )(qkv, qkv, qkv, qkv, qkv, qkv, qkv, qkv, ext, sgu, gate, h, wa, ws, wo, g, wi, wf, g_final)


def kernel(x, norm_mix, w_in, att_rel_bias, sgu_norm_gain, sgu_norm_bias, sgu_w, sgu_b, w_br_att, w_br_sgu, b_gate, w_out, norm_ffn, w_ffn_in, w_ffn_out, norm_final):
    batch, seq, d = x.shape
    depth = w_in.shape[0]
    assert d == D_MODEL and seq % ROW_BLOCK == 0
    n_steps = batch * seq // PROJ_ROWS
    h = x.reshape(batch * seq, d)
    w_in_l = w_in[0].astype(jnp.bfloat16)
    for l in range(depth):
        sgu_b_full = jnp.repeat(sgu_b[l].T, SGU_GROUP_DIM, axis=1)
        last = l == depth - 1
        casts = [_Cast(w, l, n_steps) for w in (w_br_att, w_br_sgu, w_out, w_ffn_in, w_ffn_out)]
        if not last:
            casts.append(_Cast(w_in, l + 1, n_steps))
        qkv, sgu, gate, wa, ws, wo, wi, wf, *next_w_in = _proj(
            l, h, norm_mix[l][None], w_in_l, sgu_norm_gain[l][None], sgu_norm_bias[l][None],
            sgu_w, sgu_b_full, b_gate[l].reshape(1, GATE_WIDTH), casts)
        h = _post(qkv, _extended_rel_bias(att_rel_bias[l]), sgu, gate, h, wa, ws, wo,
                  norm_ffn[l][None], wi, wf, norm_final[None], final_norm=last,
                  blocks_per_seq=seq // ROW_BLOCK)
        if not last:
            w_in_l, = next_w_in
    return h.reshape(batch, seq, d)
```
